```python
import math
import jax, jax.numpy as jnp
from jax import lax
import numpy as np

D_MODEL = 2048
BATCH = 4
SEQ = 4096
DEPTH = 1

HEAD_DIM = 128
N_NSA_HEADS = 8
N_NSA_KV = 2
NSA_GROUP = N_NSA_HEADS // N_NSA_KV
N_SB_HEADS = 8
CMP_BLOCK = 32
CMP_STRIDE = 16
CMP_HIDDEN = 256
SLC_BLOCK = 64
N_SELECT = 16
WINDOW = 512
Q_BLOCK = 128
SLC_Q_CHUNK = 64
D_FF = 5632
ROPE_THETA = 10000.0
EPS = 1e-6
NEG_INF = -1e30
FORCE_SCORE = 1e9

kernel_name = "hybrid_nsa_stickbreaking_macaron"


def rms_norm(x, g):
    xf = x.astype(jnp.float32)
    y = xf * lax.rsqrt(jnp.mean(xf * xf, axis=-1, keepdims=True) + EPS)
    return (y * g.astype(jnp.float32)).astype(x.dtype)


def apply_rope(x, cos, sin):
    x1, x2 = jnp.split(x, 2, axis=-1)
    return jnp.concatenate([x1 * cos - x2 * sin, x2 * cos + x1 * sin], axis=-1)


def swiglu(x, w_gate, w_up, w_down):
    return (jax.nn.silu(x @ w_gate) * (x @ w_up)) @ w_down


def compress_blocks(t, pos, w1, w2):
    b, s, g, d = t.shape
    n_cmp = (s - CMP_BLOCK) // CMP_STRIDE + 1
    idx = jnp.arange(n_cmp)[:, None] * CMP_STRIDE + jnp.arange(CMP_BLOCK)[None, :]
    blocks = t[:, idx] + pos[None, None, :, None, :]
    blocks = blocks.transpose(0, 1, 3, 2, 4).reshape(b, n_cmp, g, CMP_BLOCK * d)
    return jax.nn.gelu(blocks @ w1) @ w2


def nsa_attention(q, kc, vc, k_slc, v_slc, k_win, v_win, gates):
    b, s, h, d = q.shape
    g, r = N_NSA_KV, NSA_GROUP
    scale = d ** -0.5
    t = jnp.arange(s)
    qg = q.reshape(b, s, g, r, d).transpose(0, 2, 3, 1, 4)

    n_cmp = kc.shape[1]
    sc = jnp.einsum('bgrtd,bngd->bgrtn', qg, kc, preferred_element_type=jnp.float32) * scale
    blk_end = jnp.arange(n_cmp) * CMP_STRIDE + CMP_BLOCK - 1
    cvalid = blk_end[None, :] <= t[:, None]
    pc = jax.nn.softmax(jnp.where(cvalid, sc, NEG_INF), axis=-1)
    pc = jnp.where(cvalid, pc, 0.0)
    o_cmp = jnp.einsum('bgrtn,bngd->bgrtd', pc.astype(vc.dtype), vc)

    n_slc = s // SLC_BLOCK
    ratio = SLC_BLOCK // CMP_STRIDE
    span = CMP_BLOCK // CMP_STRIDE
    imp = pc.sum(axis=2)
    pad_front = span - 1
    pad_back = max(ratio * n_slc + span - 1 - (n_cmp + pad_front), 0)
    imp = jnp.pad(imp, ((0, 0), (0, 0), (0, 0), (pad_front, pad_back)))
    p_slc = imp[..., 0:ratio * n_slc:ratio]
    for off in range(1, ratio + span - 1):
        p_slc = p_slc + imp[..., off:off + ratio * n_slc:ratio]
    blk = jnp.arange(n_slc)
    cur = t // SLC_BLOCK
    forced = (blk[None, :] == 0) | (blk[None, :] == cur[:, None]) | (blk[None, :] == cur[:, None] - 1)
    causal_ok = blk[None, :] * SLC_BLOCK <= t[:, None]
    score = jnp.where(causal_ok, jnp.where(forced, FORCE_SCORE, p_slc), NEG_INF)
    n_top = min(N_SELECT, n_slc)
    _, sel = lax.top_k(score, n_top)

    kb = k_slc.reshape(b, n_slc, SLC_BLOCK, g, d).transpose(0, 3, 1, 2, 4)
    vb = v_slc.reshape(b, n_slc, SLC_BLOCK, g, d).transpose(0, 3, 1, 2, 4)
    nc = s // SLC_Q_CHUNK
    q_c = qg.reshape(b, g, r, nc, SLC_Q_CHUNK, d).transpose(3, 0, 1, 2, 4, 5)
    sel_c = sel.reshape(b, g, nc, SLC_Q_CHUNK, n_top).transpose(2, 0, 1, 3, 4)
    t_c = t.reshape(nc, SLC_Q_CHUNK)
    bi = jnp.arange(b)[:, None, None, None]
    gi = jnp.arange(g)[None, :, None, None]

    def slc_chunk(args):
        qb, ib, tb = args
        kg = kb[bi, gi, ib]
        vg = vb[bi, gi, ib]
        sc_ = jnp.einsum('bgrqd,bgqnld->bgrqnl', qb, kg, preferred_element_type=jnp.float32) * scale
        kpos = ib[..., None] * SLC_BLOCK + jnp.arange(SLC_BLOCK)
        mask = (kpos <= tb[None, None, :, None, None])[:, :, None]
        sc_ = jnp.where(mask, sc_, NEG_INF).reshape(b, g, r, SLC_Q_CHUNK, n_top * SLC_BLOCK)
        p = jax.nn.softmax(sc_, axis=-1).reshape(b, g, r, SLC_Q_CHUNK, n_top, SLC_BLOCK)
        return jnp.einsum('bgrqnl,bgqnld->bgrqd', p.astype(vg.dtype), vg)

    o_slc = lax.map(slc_chunk, (q_c, sel_c, t_c))
    o_slc = o_slc.transpose(1, 2, 3, 0, 4, 5).reshape(b, g, r, s, d)

    nq = s // Q_BLOCK
    span_w = WINDOW + Q_BLOCK
    widx = jnp.arange(nq)[:, None] * Q_BLOCK + jnp.arange(span_w)[None, :]
    kpad = jnp.pad(k_win, ((0, 0), (WINDOW, 0), (0, 0), (0, 0)))
    vpad = jnp.pad(v_win, ((0, 0), (WINDOW, 0), (0, 0), (0, 0)))
    kw = kpad[:, widx].transpose(0, 3, 1, 2, 4)
    vw = vpad[:, widx].transpose(0, 3, 1, 2, 4)
    qw = qg.reshape(b, g, r, nq, Q_BLOCK, d)
    sw = jnp.einsum('bgriqd,bgikd->bgriqk', qw, kw, preferred_element_type=jnp.float32) * scale
    kpos = widx - WINDOW
    diff = t.reshape(nq, Q_BLOCK)[:, :, None] - kpos[:, None, :]
    wmask = (kpos[:, None, :] >= 0) & (diff >= 0) & (diff < WINDOW)
    pw = jax.nn.softmax(jnp.where(wmask, sw, NEG_INF), axis=-1)
    o_win = jnp.einsum('bgriqk,bgikd->bgriqd', pw.astype(vw.dtype), vw).reshape(b, g, r, s, d)

    gt = jax.nn.sigmoid(gates.astype(jnp.float32)).astype(q.dtype)
    gt = gt.reshape(b, s, g, r, 3).transpose(0, 2, 3, 1, 4)
    o = gt[..., 0:1] * o_cmp + gt[..., 1:2] * o_slc + gt[..., 2:3] * o_win
    return o.transpose(0, 3, 1, 2, 4).reshape(b, s, h, d)


def stick_breaking_attention(q, k, v):
    b, s, h, d = q.shape
    scale = d ** -0.5
    qh, kh, vh = (a.transpose(0, 2, 1, 3) for a in (q, k, v))
    nq = s // Q_BLOCK
    q_c = qh.reshape(b, h, nq, Q_BLOCK, d).transpose(2, 0, 1, 3, 4)
    t_c = jnp.arange(s).reshape(nq, Q_BLOCK)
    kpos = jnp.arange(s)

    def sb_block(args):
        qb, tb = args
        z = jnp.einsum('bhqd,bhkd->bhqk', qb, kh, preferred_element_type=jnp.float32) * scale
        mask = kpos[None, :] < tb[:, None]
        log_keep = jnp.where(mask, jax.nn.log_sigmoid(-z), 0.0)
        after = lax.cumsum(log_keep, axis=3, reverse=True) - log_keep
        a = jnp.where(mask, jnp.exp(jax.nn.log_sigmoid(z) + after), 0.0)
        return jnp.einsum('bhqk,bhkd->bhqd', a.astype(vh.dtype), vh)

    o = lax.map(sb_block, (q_c, t_c))
    return o.transpose(1, 0, 3, 2, 4).reshape(b, s, h, d)


def setup_inputs(seed: int = 0) -> dict:
    key = jax.random.key(seed)
    ks = jax.random.split(key, 32)
    f32 = jnp.float32
    L = DEPTH
    hd = HEAD_DIM
    kv = N_NSA_KV * hd
    in_cols = N_NSA_HEADS * hd + 6 * kv + N_NSA_HEADS * 3 + 3 * N_SB_HEADS * hd
    mix = (N_NSA_HEADS + N_SB_HEADS) * hd

    def nrm(k, shape, fan_in):
        return jax.random.normal(k, shape, f32) * (fan_in ** -0.5)

    def gain(k, shape):
        return 1.0 + 0.1 * jax.random.normal(k, shape, f32)

    return {
        "x": jax.random.normal(ks[0], (BATCH, SEQ, D_MODEL), f32),
        "positions": jnp.broadcast_to(jnp.arange(SEQ, dtype=jnp.int32)[None, :], (BATCH, SEQ)),
        "ffn1_norm": gain(ks[1], (L, D_MODEL)),
        "ffn1_w_gate": nrm(ks[2], (L, D_MODEL, D_FF), D_MODEL),
        "ffn1_w_up": nrm(ks[3], (L, D_MODEL, D_FF), D_MODEL),
        "ffn1_w_down": nrm(ks[4], (L, D_FF, D_MODEL), D_FF),
        "mix_norm": gain(ks[5], (L, D_MODEL)),
        "w_in": nrm(ks[6], (L, D_MODEL, in_cols), D_MODEL),
        "nsa_q_norm": gain(ks[7], (L, hd)),
        "nsa_k_cmp_norm": gain(ks[8], (L, hd)),
        "nsa_k_slc_norm": gain(ks[9], (L, hd)),
        "nsa_k_win_norm": gain(ks[10], (L, hd)),
        "cmp_k_pos": 0.1 * jax.random.normal(ks[11], (L, CMP_BLOCK, hd), f32),
        "cmp_k_w1": nrm(ks[12], (L, CMP_BLOCK * hd, CMP_HIDDEN), CMP_BLOCK * hd),
        "cmp_k_w2": nrm(ks[13], (L, CMP_HIDDEN, hd), CMP_HIDDEN),
        "cmp_v_pos": 0.1 * jax.random.normal(ks[14], (L, CMP_BLOCK, hd), f32),
        "cmp_v_w1": nrm(ks[15], (L, CMP_BLOCK * hd, CMP_HIDDEN), CMP_BLOCK * hd),
        "cmp_v_w2": nrm(ks[16], (L, CMP_HIDDEN, hd), CMP_HIDDEN),
        "nsa_out_norm": gain(ks[17], (L, N_NSA_HEADS, hd)),
        "sb_out_norm": gain(ks[18], (L, N_SB_HEADS, hd)),
        "w_out": nrm(ks[19], (L, mix, D_MODEL), mix),
        "ffn2_norm": gain(ks[20], (L, D_MODEL)),
        "ffn2_w_gate": nrm(ks[21], (L, D_MODEL, D_FF), D_MODEL),
        "ffn2_w_up": nrm(ks[22], (L, D_MODEL, D_FF), D_MODEL),
        "ffn2_w_down": nrm(ks[23], (L, D_FF, D_MODEL), D_FF),
    }


def reference(x, positions, ffn1_norm, ffn1_w_gate, ffn1_w_up, ffn1_w_down, mix_norm, w_in,
              nsa_q_norm, nsa_k_cmp_norm, nsa_k_slc_norm, nsa_k_win_norm,
              cmp_k_pos, cmp_k_w1, cmp_k_w2, cmp_v_pos, cmp_v_w1, cmp_v_w2,
              nsa_out_norm, sb_out_norm, w_out, ffn2_norm, ffn2_w_gate, ffn2_w_up, ffn2_w_down):
    b, s, _ = x.shape
    hd = HEAD_DIM
    kv = N_NSA_KV * hd
    sizes = [N_NSA_HEADS * hd, kv, kv, kv, kv, kv, kv, N_NSA_HEADS * 3,
             N_SB_HEADS * hd, N_SB_HEADS * hd, N_SB_HEADS * hd]
    split_at = np.cumsum(sizes)[:-1].tolist()

    inv_freq = ROPE_THETA ** (-jnp.arange(0, hd, 2, dtype=jnp.float32) / hd)
    ang = positions.astype(jnp.float32)[..., None] * inv_freq
    cos = jnp.cos(ang)[:, :, None, :].astype(x.dtype)
    sin = jnp.sin(ang)[:, :, None, :].astype(x.dtype)

    for l in range(DEPTH):
        x = x + 0.5 * swiglu(rms_norm(x, ffn1_norm[l]), ffn1_w_gate[l], ffn1_w_up[l], ffn1_w_down[l])

        h = rms_norm(x, mix_norm[l])
        parts = jnp.split(h @ w_in[l], split_at, axis=-1)
        q_n, kc_t, vc_t, ks_t, vs_t, kw_t, vw_t, gate_n, q_s, k_s, v_s = parts
        kv_shape = (b, s, N_NSA_KV, hd)

        q_n = apply_rope(rms_norm(q_n.reshape(b, s, N_NSA_HEADS, hd), nsa_q_norm[l]), cos, sin)
        kc = compress_blocks(apply_rope(kc_t.reshape(kv_shape), cos, sin), cmp_k_pos[l], cmp_k_w1[l], cmp_k_w2[l])
        kc = rms_norm(kc, nsa_k_cmp_norm[l])
        vc = compress_blocks(vc_t.reshape(kv_shape), cmp_v_pos[l], cmp_v_w1[l], cmp_v_w2[l])
        k_slc = apply_rope(rms_norm(ks_t.reshape(kv_shape), nsa_k_slc_norm[l]), cos, sin)
        k_win = apply_rope(rms_norm(kw_t.reshape(kv_shape), nsa_k_win_norm[l]), cos, sin)
        o_nsa = nsa_attention(q_n, kc, vc, k_slc, vs_t.reshape(kv_shape), k_win,
                              vw_t.reshape(kv_shape), gate_n.reshape(b, s, N_NSA_HEADS, 3))

        sb_shape = (b, s, N_SB_HEADS, hd)
        o_sb = stick_breaking_attention(q_s.reshape(sb_shape), k_s.reshape(sb_shape), v_s.reshape(sb_shape))

        y = jnp.concatenate([rms_norm(o_nsa, nsa_out_norm[l]).reshape(b, s, -1),
                             rms_norm(o_sb, sb_out_norm[l]).reshape(b, s, -1)], axis=-1)
        x = x + y @ w_out[l]

        x = x + 0.5 * swiglu(rms_norm(x, ffn2_norm[l]), ffn2_w_gate[l], ffn2_w_up[l], ffn2_w_down[l])
    return x
```

```python
import functools

import jax
import jax.numpy as jnp
from jax import lax
from jax.experimental import pallas as pl
from jax.experimental.pallas import tpu as pltpu

HEAD_DIM = 128
N_NSA_HEADS = 8
N_NSA_KV = 2
NSA_GROUP = N_NSA_HEADS // N_NSA_KV
N_SB_HEADS = 8
CMP_BLOCK = 32
CMP_STRIDE = 16
SLC_BLOCK = 64
N_SELECT = 16
WINDOW = 512
ROPE_THETA = 10000.0
EPS = 1e-6
NEG_INF = -1e30
FORCE_SCORE = 1e9

LANES = 128
VMEM_LIMIT = 56 * 1024 * 1024

BLK_Q = 0
BLK_KC, BLK_VC = 8, 10
BLK_KS, BLK_VS = 12, 14
BLK_KW, BLK_VW = 16, 18
BLK_GATE = 20
BLK_QS, BLK_KSB, BLK_VSB = 24, 32, 40
N_BLK = 48
PROJ_NB = 4

BF16 = jnp.bfloat16
F32 = jnp.float32


def _nt(a, b):
    return lax.dot_general(a, b, (((1,), (1,)), ((), ())), preferred_element_type=F32)


def _mm(a, b):
    return jnp.dot(a, b, preferred_element_type=F32)


def _rms(v, g):
    return v * lax.rsqrt(jnp.mean(v * v, axis=-1, keepdims=True) + EPS) * g


def _ffn_kernel(x_ref, g_ref, wg_ref, wu_ref, wd_ref, o_ref, xn_ref):
    j = pl.program_id(1)

    @pl.when(j == 0)
    def _():
        xn_ref[...] = _rms(x_ref[...], g_ref[...]).astype(BF16)
        o_ref[...] = jnp.zeros_like(o_ref)

    xn = xn_ref[...]
    h = _mm(xn, wg_ref[...])
    u = _mm(xn, wu_ref[...])
    a = (h * (1.0 / (1.0 + jnp.exp(-h)))) * u
    o_ref[...] += _mm(a.astype(BF16), wd_ref[...])

    @pl.when(j == pl.num_programs(1) - 1)
    def _():
        o_ref[...] = x_ref[...] + 0.5 * o_ref[...]


def _ffn(x2d, gain, wg, wu, wd, tm, tf):
    t, d = x2d.shape
    ff = wg.shape[1]
    return pl.pallas_call(
        _ffn_kernel,
        grid=(t // tm, ff // tf),
        in_specs=[
            pl.BlockSpec((tm, d), lambda i, j: (i, 0)),
            pl.BlockSpec((1, d), lambda i, j: (0, 0)),
            pl.BlockSpec((d, tf), lambda i, j: (0, j)),
            pl.BlockSpec((d, tf), lambda i, j: (0, j)),
            pl.BlockSpec((tf, d), lambda i, j: (j, 0)),
        ],
        out_specs=pl.BlockSpec((tm, d), lambda i, j: (i, 0)),
        out_shape=jax.ShapeDtypeStruct((t, d), F32),
        scratch_shapes=[pltpu.VMEM((tm, d), BF16)],
        compiler_params=pltpu.CompilerParams(
            dimension_semantics=("parallel", "arbitrary"), vmem_limit_bytes=VMEM_LIMIT),
        name="ffn",
    )(x2d, gain.reshape(1, d), wg, wu, wd)


def _proj_kernel(x_ref, g_ref, w_ref, cos_ref, sin_ref, hg_ref, p_ref, c_ref, xn_ref):
    j = pl.program_id(2)
    scale = HEAD_DIM ** -0.5

    @pl.when(j == 0)
    def _():
        xn_ref[...] = _rms(x_ref[0], g_ref[...]).astype(BF16)

    acc = _mm(xn_ref[...], w_ref[...])

    def blk(c):
        return acc[:, c * LANES:(c + 1) * LANES]

    def rope(v):
        return v * cos_ref[0] + pltpu.roll(v, HEAD_DIM // 2, 1) * sin_ref[0]

    def put(c, v):
        p_ref[0, c] = v.astype(BF16)

    @pl.when(j < 2)
    def _():
        for c in range(PROJ_NB):
            put(c, rope(_rms(blk(c), hg_ref[0:1, :])) * scale)

    @pl.when(j == 2)
    def _():
        for c in range(2):
            v = rope(blk(c))
            put(c, v)
            c_ref[0, c] = v
        for c in range(2, 4):
            put(c, blk(c))
            c_ref[0, c] = blk(c)

    @pl.when(j == 3)
    def _():
        for c in range(2):
            put(c, rope(_rms(blk(c), hg_ref[1:2, :])))
        for c in range(2, 4):
            put(c, blk(c))

    @pl.when(j == 4)
    def _():
        for c in range(2):
            put(c, rope(_rms(blk(c), hg_ref[2:3, :])))
        for c in range(2, 4):
            put(c, blk(c))

    @pl.when(j == 5)
    def _():
        for c in range(PROJ_NB):
            put(c, blk(c))

    @pl.when((j == 6) | (j == 7))
    def _():
        for c in range(PROJ_NB):
            put(c, blk(c) * scale)

    @pl.when(j > 7)
    def _():
        for c in range(PROJ_NB):
            put(c, blk(c))


def _in_proj(x, gain, w, cosf, sinf, head_gains, tm):
    b, s, d = x.shape
    tn = PROJ_NB * LANES
    return pl.pallas_call(
        _proj_kernel,
        grid=(b, s // tm, N_BLK // PROJ_NB),
        in_specs=[
            pl.BlockSpec((1, tm, d), lambda bb, i, j: (bb, i, 0)),
            pl.BlockSpec((1, d), lambda bb, i, j: (0, 0)),
            pl.BlockSpec((d, tn), lambda bb, i, j: (0, j)),
            pl.BlockSpec((1, tm, LANES), lambda bb, i, j: (bb, i, 0)),
            pl.BlockSpec((1, tm, LANES), lambda bb, i, j: (bb, i, 0)),
            pl.BlockSpec((8, LANES), lambda bb, i, j: (0, 0)),
        ],
        out_specs=[
            pl.BlockSpec((1, PROJ_NB, tm, LANES), lambda bb, i, j: (bb, j, i, 0)),
            pl.BlockSpec((1, 4, tm, LANES), lambda bb, i, j: (bb, 0, i, 0)),
        ],
        out_shape=[
            jax.ShapeDtypeStruct((b, N_BLK, s, LANES), BF16),
            jax.ShapeDtypeStruct((b, 4, s, LANES), F32),
        ],
        scratch_shapes=[pltpu.VMEM((tm, d), BF16)],
        compiler_params=pltpu.CompilerParams(
            dimension_semantics=("parallel", "parallel", "arbitrary"), vmem_limit_bytes=VMEM_LIMIT),
        name="in_proj",
    )(x, gain.reshape(1, d), w, cosf, sinf, head_gains)


def _compress_kernel(x_ref, pos_ref, w1_ref, w2_ref, g_ref, o_ref, *, normalize):
    nb = o_ref.shape[2]
    hid = w2_ref.shape[0]
    acc = jnp.zeros((nb, 2 * hid), F32)
    bias = jnp.zeros((8, hid), F32)
    for l in range(CMP_STRIDE):
        rows = x_ref[0, 0, pl.ds(l, nb, stride=CMP_STRIDE), :]
        acc = acc + _mm(rows.astype(BF16), w1_ref[l])
        p_lo = jnp.broadcast_to(pos_ref[l:l + 1, :], (8, HEAD_DIM)).astype(BF16)
        p_hi = jnp.broadcast_to(pos_ref[l + CMP_STRIDE:l + CMP_STRIDE + 1, :], (8, HEAD_DIM)).astype(BF16)
        bias = bias + _mm(p_lo, w1_ref[l, :, :hid]) + _mm(p_hi, w1_ref[l, :, hid:])
    hidden = acc[:, :hid] + pltpu.roll(acc[:, hid:], nb - 1, 0) + bias[0:1, :]
    act = 0.5 * hidden * (1.0 + jnp.tanh(0.7978845608028654 * (hidden + 0.044715 * hidden * hidden * hidden)))
    out = _mm(act.astype(BF16), w2_ref[...])
    if normalize:
        out = _rms(out, g_ref[...])
    row = lax.broadcasted_iota(jnp.int32, out.shape, 0)
    o_ref[0, 0] = jnp.where(row < nb - 1, out, 0.0).astype(o_ref.dtype)


def _compress(src, first_blk, pos, w1cat, w2, gain, normalize, out_dtype):
    b, _, s, _ = src.shape
    nb = s // CMP_STRIDE
    hid = w2.shape[0]
    return pl.pallas_call(
        functools.partial(_compress_kernel, normalize=normalize),
        grid=(b, N_NSA_KV),
        in_specs=[
            pl.BlockSpec((1, 1, s, LANES), lambda bb, g: (bb, first_blk + g, 0, 0)),
            pl.BlockSpec((CMP_BLOCK, HEAD_DIM), lambda bb, g: (0, 0)),
            pl.BlockSpec((CMP_STRIDE, HEAD_DIM, 2 * hid), lambda bb, g: (0, 0, 0)),
            pl.BlockSpec((hid, HEAD_DIM), lambda bb, g: (0, 0)),
            pl.BlockSpec((1, HEAD_DIM), lambda bb, g: (0, 0)),
        ],
        out_specs=pl.BlockSpec((1, 1, nb, HEAD_DIM), lambda bb, g: (bb, g, 0, 0)),
        out_shape=jax.ShapeDtypeStruct((b, N_NSA_KV, nb, HEAD_DIM), out_dtype),
        compiler_params=pltpu.CompilerParams(
            dimension_semantics=("parallel", "parallel"), vmem_limit_bytes=VMEM_LIMIT),
        name="compress_k" if normalize else "compress_v",
    )(src, pos, w1cat, w2, gain.reshape(1, HEAD_DIM))


def _nsa_kernel(q_ref, kc_ref, vc_ref, ks_ref, vs_ref, kw_ref, vw_ref, gate_ref, e_ref, gain_ref,
                o_ref, m_ref, l_ref, acc_ref, sc_ref, *, tq, tk):
    i = pl.program_id(2)
    r = NSA_GROUP
    mrows = r * tq
    q0 = i * tq
    ncb = kc_ref.shape[2]
    nsb = LANES

    q = q_ref[0].reshape(mrows, HEAD_DIM)
    t_row = q0 + lax.broadcasted_iota(jnp.int32, (tq, 1), 0)

    s = _nt(q, kc_ref[0, 0].astype(BF16)).reshape(r, tq, ncb)
    cend = lax.broadcasted_iota(jnp.int32, (tq, ncb), 1) * CMP_STRIDE + (CMP_BLOCK - 1)
    cvalid = (cend <= t_row)[None]
    s = jnp.where(cvalid, s, NEG_INF)
    mx = jnp.max(s, axis=-1, keepdims=True)
    e = jnp.where(cvalid, jnp.exp(s - mx), 0.0)
    den = jnp.sum(e, axis=-1, keepdims=True)
    pc = e / jnp.where(den > 0.0, den, 1.0)
    o_cmp = _mm(pc.reshape(mrows, ncb).astype(BF16), vc_ref[0, 0])

    imp = jnp.sum(pc, axis=0)
    imp_hi = imp.astype(BF16)
    imp_lo = (imp - imp_hi.astype(F32)).astype(BF16)
    brow = lax.broadcasted_iota(jnp.int32, (nsb, ncb), 0)
    ccol = lax.broadcasted_iota(jnp.int32, (nsb, ncb), 1)
    ratio = SLC_BLOCK // CMP_STRIDE
    span = CMP_BLOCK // CMP_STRIDE
    cover = ((ccol >= ratio * brow - (span - 1)) & (ccol <= ratio * brow + ratio - 1)
             & (ccol < ncb - 1)).astype(BF16)
    p_slc = _nt(cover, imp_hi) + _nt(cover, imp_lo)
    bidx = lax.broadcasted_iota(jnp.int32, (nsb, tq), 0)
    t_lane = q0 + lax.broadcasted_iota(jnp.int32, (nsb, tq), 1)
    cur = t_lane // SLC_BLOCK
    forced = (bidx == 0) | (bidx == cur) | (bidx == cur - 1)
    score = jnp.where(bidx <= cur, jnp.where(forced, FORCE_SCORE, p_slc), NEG_INF)
    n_slc = ks_ref.shape[2] // SLC_BLOCK
    score = jnp.where(bidx < n_slc, score, -3e38)
    sc_ref[...] = score
    rank = jnp.zeros((nsb, tq), F32)
    for bp in range(n_slc):
        row = sc_ref[bp:bp + 1, :]
        ge = jnp.where(row >= score, 1.0, 0.0)
        gt = jnp.where(row > score, 1.0, 0.0)
        rank = rank + jnp.where(bidx > bp, ge, gt)
    sel_t = ((rank < float(min(N_SELECT, n_slc))) & (bidx < n_slc)).astype(F32)
    sel = jnp.transpose(sel_t).astype(BF16)

    def init():
        m_ref[...] = jnp.full(m_ref.shape, NEG_INF, F32)
        l_ref[...] = jnp.zeros(l_ref.shape, F32)
        acc_ref[...] = jnp.zeros(acc_ref.shape, F32)

    def flash(k, v, allowed):
        width = k.shape[0]
        sk = _nt(q, k).reshape(r, tq, width)
        sk = jnp.where(allowed[None], sk, NEG_INF).reshape(mrows, width)
        m_prev = m_ref[...]
        m_next = jnp.maximum(m_prev, jnp.max(sk, axis=-1, keepdims=True))
        p = jnp.exp(sk - jnp.tile(m_next, (1, width // LANES)))
        p = jnp.where(jnp.broadcast_to(allowed[None], (r, tq, width)).reshape(mrows, width), p, 0.0)
        alpha = jnp.exp(m_prev - m_next)
        l_ref[...] = alpha * l_ref[...] + jnp.sum(p, axis=-1, keepdims=True)
        acc_ref[...] = alpha * acc_ref[...] + _mm(p.astype(BF16), v)
        m_ref[...] = m_next

    def finish():
        return acc_ref[...] / l_ref[...]

    init()

    def slc_body(jt, carry):
        k0 = pl.multiple_of(jt * tk, tk)
        k = ks_ref[0, 0, pl.ds(k0, tk), :]
        v = vs_ref[0, 0, pl.ds(k0, tk), :]
        chosen = _mm(sel, e_ref[jt]) > 0.5
        kpos = k0 + lax.broadcasted_iota(jnp.int32, (tq, tk), 1)
        flash(k, v, chosen & (kpos <= t_row))
        return carry

    lax.fori_loop(0, (q0 + tq + tk - 1) // tk, slc_body, 0)
    o_slc = finish()

    init()
    nwin = WINDOW // tq + 1

    def win_body(w, carry):
        k0 = pl.multiple_of(q0 - w * tq, tq)
        k = kw_ref[0, 0, pl.ds(k0, tq), :]
        v = vw_ref[0, 0, pl.ds(k0, tq), :]
        diff = t_row - (k0 + lax.broadcasted_iota(jnp.int32, (tq, tq), 1))
        flash(k, v, (diff >= 0) & (diff < WINDOW))
        return carry

    lax.fori_loop(0, jnp.minimum(nwin, i + 1), win_body, 0)
    o_win = finish()

    gt = 1.0 / (1.0 + jnp.exp(-gate_ref[0, 0].astype(F32)))
    for h in range(r):
        rows = slice(h * tq, (h + 1) * tq)
        o = (gt[:, 3 * h:3 * h + 1] * o_cmp[rows] + gt[:, 3 * h + 1:3 * h + 2] * o_slc[rows]
             + gt[:, 3 * h + 2:3 * h + 3] * o_win[rows])
        o_ref[0, :, h * HEAD_DIM:(h + 1) * HEAD_DIM] = _rms(o, gain_ref[0, h:h + 1, :]).astype(BF16)


def _nsa(p, kc, vc, expand, gains, tq, tk):
    b, _, s, _ = p.shape
    g, r = N_NSA_KV, NSA_GROUP
    ncb = kc.shape[2]

    def kv_spec(first):
        return pl.BlockSpec((1, 1, s, LANES), lambda bb, gg, i: (bb, first + gg, 0, 0))

    return pl.pallas_call(
        functools.partial(_nsa_kernel, tq=tq, tk=tk),
        grid=(b, g, s // tq),
        in_specs=[
            pl.BlockSpec((1, r, tq, LANES), lambda bb, gg, i: (bb, gg, i, 0)),
            pl.BlockSpec((1, 1, ncb, LANES), lambda bb, gg, i: (bb, gg, 0, 0)),
            pl.BlockSpec((1, 1, ncb, LANES), lambda bb, gg, i: (bb, gg, 0, 0)),
            kv_spec(BLK_KS), kv_spec(BLK_VS), kv_spec(BLK_KW), kv_spec(BLK_VW),
            pl.BlockSpec((1, 1, tq, LANES), lambda bb, gg, i: (bb, BLK_GATE + gg, i, 0)),
            pl.BlockSpec((s // tk, LANES, tk), lambda bb, gg, i: (0, 0, 0)),
            pl.BlockSpec((1, r, HEAD_DIM), lambda bb, gg, i: (gg, 0, 0)),
        ],
        out_specs=pl.BlockSpec((1, tq, r * HEAD_DIM), lambda bb, gg, i: (bb, i, gg)),
        out_shape=jax.ShapeDtypeStruct((b, s, N_NSA_HEADS * HEAD_DIM), BF16),
        scratch_shapes=[
            pltpu.VMEM((r * tq, LANES), F32),
            pltpu.VMEM((r * tq, LANES), F32),
            pltpu.VMEM((r * tq, HEAD_DIM), F32),
            pltpu.VMEM((LANES, tq), F32),
        ],
        compiler_params=pltpu.CompilerParams(
            dimension_semantics=("parallel", "parallel", "arbitrary"), vmem_limit_bytes=VMEM_LIMIT),
        name="nsa",
    )(p, kc, vc, p, p, p, p, p, expand, gains)


def _sb_kernel(q_ref, k_ref, v_ref, tri_ref, gain_ref, o_ref, acc_ref, c_ref, *, tq):
    i = pl.program_id(2)
    q = q_ref[0, 0]
    acc_ref[...] = jnp.zeros(acc_ref.shape, F32)
    c_ref[...] = jnp.zeros(c_ref.shape, F32)
    tri = tri_ref[...]

    def tile(k0, mask):
        k = k_ref[0, 0, pl.ds(k0, tq), :]
        v = v_ref[0, 0, pl.ds(k0, tq), :]
        z = _nt(q, k)
        lk = -(jnp.maximum(z, 0.0) + jnp.log(1.0 + jnp.exp(-jnp.abs(z))))
        if mask is not None:
            lk = jnp.where(mask, lk, 0.0)
        hi = lk.astype(BF16)
        lo = (lk - hi.astype(F32)).astype(BF16)
        after = _mm(hi, tri) + _mm(lo, tri) + jnp.tile(c_ref[...], (1, tq // LANES))
        a = jnp.exp(lk + z + after)
        if mask is not None:
            a = jnp.where(mask, a, 0.0)
        acc_ref[...] += _mm(a.astype(BF16), v)
        c_ref[...] += jnp.sum(lk, axis=-1, keepdims=True)

    q0 = pl.multiple_of(i * tq, tq)
    row = lax.broadcasted_iota(jnp.int32, (tq, tq), 0)
    col = lax.broadcasted_iota(jnp.int32, (tq, tq), 1)
    tile(q0, col < row)

    def body(jj, carry):
        tile(pl.multiple_of((i - 1 - jj) * tq, tq), None)
        return carry

    lax.fori_loop(0, i, body, 0)
    o_ref[0] = _rms(acc_ref[...], gain_ref[0]).astype(BF16)


def _sb(p, tri, gains, tq):
    b, _, s, _ = p.shape
    return pl.pallas_call(
        functools.partial(_sb_kernel, tq=tq),
        grid=(b, N_SB_HEADS, s // tq),
        in_specs=[
            pl.BlockSpec((1, 1, tq, LANES), lambda bb, h, i: (bb, BLK_QS + h, i, 0)),
            pl.BlockSpec((1, 1, s, LANES), lambda bb, h, i: (bb, BLK_KSB + h, 0, 0)),
            pl.BlockSpec((1, 1, s, LANES), lambda bb, h, i: (bb, BLK_VSB + h, 0, 0)),
            pl.BlockSpec((tq, tq), lambda bb, h, i: (0, 0)),
            pl.BlockSpec((1, 1, HEAD_DIM), lambda bb, h, i: (h, 0, 0)),
        ],
        out_specs=pl.BlockSpec((1, tq, HEAD_DIM), lambda bb, h, i: (bb, i, h)),
        out_shape=jax.ShapeDtypeStruct((b, s, N_SB_HEADS * HEAD_DIM), BF16),
        scratch_shapes=[
            pltpu.VMEM((tq, HEAD_DIM), F32),
            pltpu.VMEM((tq, LANES), F32),
        ],
        compiler_params=pltpu.CompilerParams(
            dimension_semantics=("parallel", "parallel", "arbitrary"), vmem_limit_bytes=VMEM_LIMIT),
        name="sb",
    )(p, p, p, tri, gains)


def _out_kernel(x_ref, ya_ref, yb_ref, wa_ref, wb_ref, o_ref):
    o_ref[...] = x_ref[...] + _mm(ya_ref[...], wa_ref[...]) + _mm(yb_ref[...], wb_ref[...])


def _out_proj(x2d, ya, yb, wa, wb, tm):
    t, d = x2d.shape
    ka, kb = ya.shape[1], yb.shape[1]
    return pl.pallas_call(
        _out_kernel,
        grid=(t // tm,),
        in_specs=[
            pl.BlockSpec((tm, d), lambda i: (i, 0)),
            pl.BlockSpec((tm, ka), lambda i: (i, 0)),
            pl.BlockSpec((tm, kb), lambda i: (i, 0)),
            pl.BlockSpec((ka, d), lambda i: (0, 0)),
            pl.BlockSpec((kb, d), lambda i: (0, 0)),
        ],
        out_specs=pl.BlockSpec((tm, d), lambda i: (i, 0)),
        out_shape=jax.ShapeDtypeStruct((t, d), F32),
        compiler_params=pltpu.CompilerParams(
            dimension_semantics=("parallel",), vmem_limit_bytes=VMEM_LIMIT),
        name="out_proj",
    )(x2d, ya, yb, wa, wb)


def _tile(n, pref):
    t = min(n, pref)
    while n % t:
        t //= 2
    return t


def _pack_w_in(w_in):
    d = w_in.shape[0]
    hd = HEAD_DIM
    kv = N_NSA_KV * hd
    nq = N_NSA_HEADS * hd
    o_gate = nq + 6 * kv
    o_sb = o_gate + N_NSA_HEADS * 3
    gate = w_in[:, o_gate:o_sb].reshape(d, N_NSA_KV, NSA_GROUP * 3)
    gate = jnp.pad(gate, ((0, 0), (0, 0), (0, hd - NSA_GROUP * 3))).reshape(d, N_NSA_KV * hd)
    pad = jnp.zeros((d, (BLK_QS - BLK_GATE - N_NSA_KV) * hd), w_in.dtype)
    return jnp.concatenate([w_in[:, :o_gate], gate, pad, w_in[:, o_sb:]], axis=1).astype(BF16)


def kernel(x, positions, ffn1_norm, ffn1_w_gate, ffn1_w_up, ffn1_w_down, mix_norm, w_in,
           nsa_q_norm, nsa_k_cmp_norm, nsa_k_slc_norm, nsa_k_win_norm,
           cmp_k_pos, cmp_k_w1, cmp_k_w2, cmp_v_pos, cmp_v_w1, cmp_v_w2,
           nsa_out_norm, sb_out_norm, w_out, ffn2_norm, ffn2_w_gate, ffn2_w_up, ffn2_w_down):
    b, s, d = x.shape
    depth = w_in.shape[0]
    hd = HEAD_DIM
    hid = cmp_k_w2.shape[1]
    t = b * s
    tm = _tile(t, 512)
    tf = _tile(ffn1_w_gate.shape[2], 512)
    tq_nsa, tk_nsa = 128, 256
    tq_sb = 256
    assert s % 256 == 0 and s // SLC_BLOCK <= LANES and WINDOW % tq_nsa == 0

    inv_freq = ROPE_THETA ** (-jnp.arange(0, hd, 2, dtype=F32) / hd)
    ang = positions.astype(F32)[..., None] * inv_freq
    cosf = jnp.concatenate([jnp.cos(ang), jnp.cos(ang)], axis=-1)
    sinf = jnp.concatenate([-jnp.sin(ang), jnp.sin(ang)], axis=-1)

    blk_of_key = jnp.arange(s, dtype=jnp.int32) // SLC_BLOCK
    expand = (jnp.arange(LANES, dtype=jnp.int32)[:, None] == blk_of_key[None, :]).astype(BF16)
    expand = expand.reshape(LANES, s // tk_nsa, tk_nsa).transpose(1, 0, 2)
    idx = jnp.arange(tq_sb, dtype=jnp.int32)
    tri = (idx[:, None] > idx[None, :]).astype(BF16)

    def w1cat(w1):
        w = w1.reshape(CMP_BLOCK, hd, hid)
        return jnp.concatenate([w[:CMP_STRIDE], w[CMP_STRIDE:]], axis=-1).astype(BF16)

    for l in range(depth):
        x2 = _ffn(x.reshape(t, d), ffn1_norm[l], ffn1_w_gate[l].astype(BF16), ffn1_w_up[l].astype(BF16),
                  ffn1_w_down[l].astype(BF16), tm, tf)

        head_gains = jnp.zeros((8, hd), F32)
        head_gains = head_gains.at[0].set(nsa_q_norm[l]).at[1].set(nsa_k_slc_norm[l]).at[2].set(nsa_k_win_norm[l])
        p, csrc = _in_proj(x2.reshape(b, s, d), mix_norm[l], _pack_w_in(w_in[l]), cosf, sinf, head_gains, tm=_tile(s, 512))

        kc = _compress(csrc, 0, cmp_k_pos[l], w1cat(cmp_k_w1[l]), cmp_k_w2[l].astype(BF16),
                       nsa_k_cmp_norm[l], True, F32)
        vc = _compress(csrc, N_NSA_KV, cmp_v_pos[l], w1cat(cmp_v_w1[l]), cmp_v_w2[l].astype(BF16),
                       jnp.ones((hd,), F32), False, BF16)

        y_nsa = _nsa(p, kc, vc, expand, nsa_out_norm[l].reshape(N_NSA_KV, NSA_GROUP, hd), tq_nsa, tk_nsa)
        y_sb = _sb(p, tri, sb_out_norm[l].reshape(N_SB_HEADS, 1, hd), tq_sb)

        nmix = N_NSA_HEADS * hd
        x3 = _out_proj(x2, y_nsa.reshape(t, nmix), y_sb.reshape(t, N_SB_HEADS * hd),
                       w_out[l, :nmix].astype(BF16), w_out[l, nmix:].astype(BF16), tm)

        x = _ffn(x3, ffn2_norm[l], ffn2_w_gate[l].astype(BF16), ffn2_w_up[l].astype(BF16),
                 ffn2_w_down[l].astype(BF16), tm, tf).reshape(b, s, d)
    return x
```

```python
import functools

import jax
import jax.numpy as jnp
from jax import lax
from jax.experimental import pallas as pl
from jax.experimental.pallas import tpu as pltpu

HEAD_DIM = 128
N_NSA_HEADS = 8
N_NSA_KV = 2
NSA_GROUP = N_NSA_HEADS // N_NSA_KV
N_SB_HEADS = 8
CMP_BLOCK = 32
CMP_STRIDE = 16
SLC_BLOCK = 64
N_SELECT = 16
WINDOW = 512
ROPE_THETA = 10000.0
EPS = 1e-6
NEG_INF = -1e30
FORCE_SCORE = 1e9
LOG2E = 1.4426950408889634
MASK_SCORE = 1e30

LANES = 128
VMEM_LIMIT = 56 * 1024 * 1024

BLK_Q = 0
BLK_KC, BLK_VC = 8, 10
BLK_KS, BLK_VS = 12, 14
BLK_KW, BLK_VW = 16, 18
BLK_GATE = 20
BLK_QS, BLK_KSB, BLK_VSB = 24, 32, 40
N_BLK = 48
PROJ_NB = 4

BF16 = jnp.bfloat16
F32 = jnp.float32


def _nt(a, b):
    return lax.dot_general(a, b, (((1,), (1,)), ((), ())), preferred_element_type=F32)


def _mm(a, b):
    return jnp.dot(a, b, preferred_element_type=F32)


def _rms(v, g):
    return v * lax.rsqrt(jnp.mean(v * v, axis=-1, keepdims=True) + EPS) * g


def _ffn_kernel(x_ref, g_ref, wg_ref, wu_ref, wd_ref, o_ref, xn_ref):
    j = pl.program_id(1)

    @pl.when(j == 0)
    def _():
        xn_ref[...] = _rms(x_ref[...], g_ref[...]).astype(BF16)
        o_ref[...] = jnp.zeros_like(o_ref)

    xn = xn_ref[...]
    h = _mm(xn, wg_ref[...])
    u = _mm(xn, wu_ref[...])
    a = (h * (1.0 / (1.0 + jnp.exp(-h)))) * u
    o_ref[...] += _mm(a.astype(BF16), wd_ref[...])

    @pl.when(j == pl.num_programs(1) - 1)
    def _():
        o_ref[...] = x_ref[...] + 0.5 * o_ref[...]


def _ffn(x2d, gain, wg, wu, wd, tm, tf):
    t, d = x2d.shape
    ff = wg.shape[1]
    return pl.pallas_call(
        _ffn_kernel,
        grid=(t // tm, ff // tf),
        in_specs=[
            pl.BlockSpec((tm, d), lambda i, j: (i, 0)),
            pl.BlockSpec((1, d), lambda i, j: (0, 0)),
            pl.BlockSpec((d, tf), lambda i, j: (0, j)),
            pl.BlockSpec((d, tf), lambda i, j: (0, j)),
            pl.BlockSpec((tf, d), lambda i, j: (j, 0)),
        ],
        out_specs=pl.BlockSpec((tm, d), lambda i, j: (i, 0)),
        out_shape=jax.ShapeDtypeStruct((t, d), F32),
        scratch_shapes=[pltpu.VMEM((tm, d), BF16)],
        compiler_params=pltpu.CompilerParams(
            dimension_semantics=("parallel", "arbitrary"), vmem_limit_bytes=VMEM_LIMIT),
        name="ffn",
    )(x2d, gain.reshape(1, d), wg, wu, wd)


def _proj_kernel(x_ref, g_ref, w_ref, cos_ref, sin_ref, hg_ref, p_ref, c_ref, xn_ref):
    j = pl.program_id(2)
    scale = HEAD_DIM ** -0.5

    @pl.when(j == 0)
    def _():
        xn_ref[...] = _rms(x_ref[0], g_ref[...]).astype(BF16)

    acc = _mm(xn_ref[...], w_ref[...])

    def blk(c):
        return acc[:, c * LANES:(c + 1) * LANES]

    def rope(v):
        return v * cos_ref[0] + pltpu.roll(v, HEAD_DIM // 2, 1) * sin_ref[0]

    def put(c, v):
        p_ref[0, c] = v.astype(BF16)

    @pl.when(j < 2)
    def _():
        for c in range(PROJ_NB):
            put(c, rope(_rms(blk(c), hg_ref[0:1, :])) * (scale * LOG2E))

    @pl.when(j == 2)
    def _():
        for c in range(2):
            v = rope(blk(c))
            put(c, v)
            c_ref[0, c] = v
        for c in range(2, 4):
            put(c, blk(c))
            c_ref[0, c] = blk(c)

    @pl.when(j == 3)
    def _():
        for c in range(2):
            put(c, rope(_rms(blk(c), hg_ref[1:2, :])))
        for c in range(2, 4):
            put(c, blk(c))

    @pl.when(j == 4)
    def _():
        for c in range(2):
            put(c, rope(_rms(blk(c), hg_ref[2:3, :])))
        for c in range(2, 4):
            put(c, blk(c))

    @pl.when(j == 5)
    def _():
        for c in range(PROJ_NB):
            put(c, blk(c))

    @pl.when((j == 6) | (j == 7))
    def _():
        for c in range(PROJ_NB):
            put(c, blk(c) * (-scale * LOG2E))

    @pl.when(j > 7)
    def _():
        for c in range(PROJ_NB):
            put(c, blk(c))


def _in_proj(x, gain, w, cosf, sinf, head_gains, tm):
    b, s, d = x.shape
    tn = PROJ_NB * LANES
    return pl.pallas_call(
        _proj_kernel,
        grid=(b, s // tm, N_BLK // PROJ_NB),
        in_specs=[
            pl.BlockSpec((1, tm, d), lambda bb, i, j: (bb, i, 0)),
            pl.BlockSpec((1, d), lambda bb, i, j: (0, 0)),
            pl.BlockSpec((d, tn), lambda bb, i, j: (0, j)),
            pl.BlockSpec((1, tm, LANES), lambda bb, i, j: (bb, i, 0)),
            pl.BlockSpec((1, tm, LANES), lambda bb, i, j: (bb, i, 0)),
            pl.BlockSpec((8, LANES), lambda bb, i, j: (0, 0)),
        ],
        out_specs=[
            pl.BlockSpec((1, PROJ_NB, tm, LANES), lambda bb, i, j: (bb, j, i, 0)),
            pl.BlockSpec((1, 4, tm, LANES), lambda bb, i, j: (bb, 0, i, 0)),
        ],
        out_shape=[
            jax.ShapeDtypeStruct((b, N_BLK, s, LANES), BF16),
            jax.ShapeDtypeStruct((b, 4, s, LANES), F32),
        ],
        scratch_shapes=[pltpu.VMEM((tm, d), BF16)],
        compiler_params=pltpu.CompilerParams(
            dimension_semantics=("parallel", "parallel", "arbitrary"), vmem_limit_bytes=VMEM_LIMIT),
        name="in_proj",
    )(x, gain.reshape(1, d), w, cosf, sinf, head_gains)


def _compress_kernel(x_ref, pos_ref, w1_ref, w2_ref, g_ref, o_ref, *, normalize):
    nb = o_ref.shape[2]
    hid = w2_ref.shape[0]
    acc = jnp.zeros((nb, 2 * hid), F32)
    bias = jnp.zeros((8, hid), F32)
    for l in range(CMP_STRIDE):
        rows = x_ref[0, 0, pl.ds(l, nb, stride=CMP_STRIDE), :]
        acc = acc + _mm(rows.astype(BF16), w1_ref[l])
        p_lo = jnp.broadcast_to(pos_ref[l:l + 1, :], (8, HEAD_DIM)).astype(BF16)
        p_hi = jnp.broadcast_to(pos_ref[l + CMP_STRIDE:l + CMP_STRIDE + 1, :], (8, HEAD_DIM)).astype(BF16)
        bias = bias + _mm(p_lo, w1_ref[l, :, :hid]) + _mm(p_hi, w1_ref[l, :, hid:])
    hidden = acc[:, :hid] + pltpu.roll(acc[:, hid:], nb - 1, 0) + bias[0:1, :]
    act = 0.5 * hidden * (1.0 + jnp.tanh(0.7978845608028654 * (hidden + 0.044715 * hidden * hidden * hidden)))
    out = _mm(act.astype(BF16), w2_ref[...])
    if normalize:
        out = _rms(out, g_ref[...])
    row = lax.broadcasted_iota(jnp.int32, out.shape, 0)
    o_ref[0, 0] = jnp.where(row < nb - 1, out, 0.0).astype(o_ref.dtype)


def _compress(src, first_blk, pos, w1cat, w2, gain, normalize, out_dtype):
    b, _, s, _ = src.shape
    nb = s // CMP_STRIDE
    hid = w2.shape[0]
    return pl.pallas_call(
        functools.partial(_compress_kernel, normalize=normalize),
        grid=(b, N_NSA_KV),
        in_specs=[
            pl.BlockSpec((1, 1, s, LANES), lambda bb, g: (bb, first_blk + g, 0, 0)),
            pl.BlockSpec((CMP_BLOCK, HEAD_DIM), lambda bb, g: (0, 0)),
            pl.BlockSpec((CMP_STRIDE, HEAD_DIM, 2 * hid), lambda bb, g: (0, 0, 0)),
            pl.BlockSpec((hid, HEAD_DIM), lambda bb, g: (0, 0)),
            pl.BlockSpec((1, HEAD_DIM), lambda bb, g: (0, 0)),
        ],
        out_specs=pl.BlockSpec((1, 1, nb, HEAD_DIM), lambda bb, g: (bb, g, 0, 0)),
        out_shape=jax.ShapeDtypeStruct((b, N_NSA_KV, nb, HEAD_DIM), out_dtype),
        compiler_params=pltpu.CompilerParams(
            dimension_semantics=("parallel", "parallel"), vmem_limit_bytes=VMEM_LIMIT),
        name="compress_k" if normalize else "compress_v",
    )(src, pos, w1cat, w2, gain.reshape(1, HEAD_DIM))


def _nsa_kernel(q_ref, kc_ref, vc_ref, ks_ref, vs_ref, kw_ref, vw_ref, gate_ref, gain_ref,
                o_ref, kaug_ref, vaug_ref, vwaug_ref, qaug_ref, m_ref, acc_ref, mix_ref, sc_ref, *, tq):
    i = pl.program_id(2)
    r = NSA_GROUP
    q0 = i * tq
    s_len = ks_ref.shape[2]
    ncb = kc_ref.shape[2]
    n_slc = s_len // SLC_BLOCK
    nsr = -(-n_slc // 8) * 8
    heads = range(r)

    @pl.when(i == 0)
    def _():
        lane = lax.broadcasted_iota(jnp.int32, (s_len, LANES), 1)
        kblk = lax.broadcasted_iota(jnp.int32, (s_len, LANES), 0) // SLC_BLOCK
        kaug_ref[:, :HEAD_DIM] = ks_ref[0, 0]
        kaug_ref[:, HEAD_DIM:] = jnp.where(lane == kblk, MASK_SCORE, 0.0).astype(BF16)
        ones = jnp.ones((s_len, LANES), BF16)
        vaug_ref[:, :HEAD_DIM] = vs_ref[0, 0]
        vaug_ref[:, HEAD_DIM:] = ones
        vwaug_ref[:, :HEAD_DIM] = vw_ref[0, 0]
        vwaug_ref[:, HEAD_DIM:] = ones

    row = lax.broadcasted_iota(jnp.int32, (tq, tq), 0)
    col = lax.broadcasted_iota(jnp.int32, (tq, tq), 1)
    causal = col <= row
    gt = 1.0 / (1.0 + jnp.exp(-gate_ref[0, 0].astype(F32)))

    def gate(h, c):
        return gt[:, 3 * h + c:3 * h + c + 1]

    kc = kc_ref[0, 0].astype(BF16)
    vc = vc_ref[0, 0]
    cend = lax.broadcasted_iota(jnp.int32, (tq, ncb), 1) * CMP_STRIDE + (CMP_BLOCK - 1)
    cvalid = cend <= q0 + lax.broadcasted_iota(jnp.int32, (tq, ncb), 0)
    imp = jnp.zeros((tq, ncb), F32)
    for h in heads:
        s = jnp.where(cvalid, _nt(q_ref[0, h], kc), NEG_INF)
        e = jnp.where(cvalid, jnp.exp2(s - jnp.max(s, axis=-1, keepdims=True)), 0.0)
        den = jnp.sum(e, axis=-1, keepdims=True)
        pc = e / jnp.where(den > 0.0, den, 1.0)
        imp = imp + pc
        mix_ref[h * tq:(h + 1) * tq, :] = gate(h, 0) * _mm(pc.astype(BF16), vc)

    nsb = nsr
    imp_hi = imp.astype(BF16)
    imp_lo = (imp - imp_hi.astype(F32)).astype(BF16)
    brow = lax.broadcasted_iota(jnp.int32, (nsb, ncb), 0)
    ccol = lax.broadcasted_iota(jnp.int32, (nsb, ncb), 1)
    ratio = SLC_BLOCK // CMP_STRIDE
    span = CMP_BLOCK // CMP_STRIDE
    cover = ((ccol >= ratio * brow - (span - 1)) & (ccol <= ratio * brow + ratio - 1)
             & (ccol < ncb - 1)).astype(BF16)
    p_slc = _nt(cover, imp_hi) + _nt(cover, imp_lo)
    bidx = lax.broadcasted_iota(jnp.int32, (nsb, tq), 0)
    t_lane = q0 + lax.broadcasted_iota(jnp.int32, (nsb, tq), 1)
    cur = t_lane // SLC_BLOCK
    forced = (bidx == 0) | (bidx == cur) | (bidx == cur - 1)
    score = jnp.where(bidx <= cur, jnp.where(forced, FORCE_SCORE, p_slc), NEG_INF)
    score = jnp.where(bidx < n_slc, score, -3e38)
    sc_ref[...] = score
    ranks = []
    for c0 in range(0, nsb, 8):
        mine = score[c0:c0 + 8, :]
        brow8 = c0 + lax.broadcasted_iota(jnp.int32, (8, tq), 0)
        rank = jnp.zeros((8, tq), F32)
        for bp in range(n_slc):
            other = sc_ref[bp:bp + 1, :]
            ge = jnp.where(other >= mine, 1.0, 0.0)
            gt_ = jnp.where(other > mine, 1.0, 0.0)
            if bp < c0:
                rank = rank + ge
            elif bp >= c0 + 8:
                rank = rank + gt_
            else:
                rank = rank + jnp.where(brow8 > bp, ge, gt_)
        ranks.append(rank)
    rank = jnp.concatenate(ranks, axis=0)
    sel_t = jnp.where((rank < float(min(N_SELECT, n_slc))) & (bidx <= cur) & (bidx < n_slc), 0.0, -1.0)
    if nsb < LANES:
        sel_t = jnp.concatenate([sel_t, jnp.full((LANES - nsb, tq), -1.0, F32)], axis=0)
    selm1 = jnp.transpose(sel_t).astype(BF16)
    for h in heads:
        qaug_ref[h * tq:(h + 1) * tq, :HEAD_DIM] = q_ref[0, h]
        qaug_ref[h * tq:(h + 1) * tq, HEAD_DIM:] = selm1

    def init():
        m_ref[...] = jnp.full(m_ref.shape, NEG_INF, F32)
        acc_ref[...] = jnp.zeros(acc_ref.shape, F32)

    def flash(q_of, k, v, mask):
        s = [_nt(q_of(h), k) for h in heads]
        p, alpha = [], []
        for h in heads:
            rows = pl.ds(h * tq, tq)
            sh = s[h] if mask is None else jnp.where(mask, s[h], NEG_INF)
            m_prev = m_ref[rows, :]
            m_next = jnp.maximum(m_prev, jnp.max(sh, axis=-1, keepdims=True))
            p.append(jnp.exp2(sh - jnp.tile(m_next, (1, tq // LANES))).astype(BF16))
            alpha.append(jnp.exp2(m_prev - m_next))
            m_ref[rows, :] = m_next
        for h in heads:
            rows = pl.ds(h * tq, tq)
            acc_ref[rows, :] = jnp.tile(alpha[h], (1, 2)) * acc_ref[rows, :] + _mm(p[h], v)

    def result(h):
        rows = pl.ds(h * tq, tq)
        return acc_ref[rows, :HEAD_DIM] / acc_ref[rows, HEAD_DIM:]

    def kv_rows(j):
        return pl.ds(pl.multiple_of(j * tq, tq), tq)

    init()

    def q_slc(h):
        return qaug_ref[h * tq:(h + 1) * tq, :]

    def slc_body(j, carry):
        flash(q_slc, kaug_ref[kv_rows(j), :], vaug_ref[kv_rows(j), :], None)
        return carry

    lax.fori_loop(0, i, slc_body, 0)
    flash(q_slc, kaug_ref[kv_rows(i), :], vaug_ref[kv_rows(i), :], causal)
    for h in heads:
        mix_ref[h * tq:(h + 1) * tq, :] += gate(h, 1) * result(h)

    init()

    def q_win(h):
        return q_ref[0, h]

    def win_tile(j, mask):
        flash(q_win, kw_ref[0, 0, kv_rows(j), :], vwaug_ref[kv_rows(j), :], mask)

    win_tile(i, causal)
    nfull = WINDOW // tq - 1

    def win_body(w, carry):
        win_tile(i - 1 - w, None)
        return carry

    lax.fori_loop(0, jnp.minimum(nfull, i), win_body, 0)

    @pl.when(i > nfull)
    def _():
        win_tile(i - 1 - nfull, jnp.logical_not(causal))

    for h in heads:
        o = mix_ref[h * tq:(h + 1) * tq, :] + gate(h, 2) * result(h)
        o_ref[0, :, h * HEAD_DIM:(h + 1) * HEAD_DIM] = _rms(o, gain_ref[0, h:h + 1, :]).astype(BF16)


def _nsa(p, kc, vc, gains, tq):
    b, _, s, _ = p.shape
    g, r = N_NSA_KV, NSA_GROUP
    ncb = kc.shape[2]
    nsr = -(-(s // SLC_BLOCK) // 8) * 8

    def kv_spec(first):
        return pl.BlockSpec((1, 1, s, LANES), lambda bb, gg, i: (bb, first + gg, 0, 0))

    return pl.pallas_call(
        functools.partial(_nsa_kernel, tq=tq),
        grid=(b, g, s // tq),
        in_specs=[
            pl.BlockSpec((1, r, tq, LANES), lambda bb, gg, i: (bb, gg, i, 0)),
            pl.BlockSpec((1, 1, ncb, LANES), lambda bb, gg, i: (bb, gg, 0, 0)),
            pl.BlockSpec((1, 1, ncb, LANES), lambda bb, gg, i: (bb, gg, 0, 0)),
            kv_spec(BLK_KS), kv_spec(BLK_VS), kv_spec(BLK_KW), kv_spec(BLK_VW),
            pl.BlockSpec((1, 1, tq, LANES), lambda bb, gg, i: (bb, BLK_GATE + gg, i, 0)),
            pl.BlockSpec((1, r, HEAD_DIM), lambda bb, gg, i: (gg, 0, 0)),
        ],
        out_specs=pl.BlockSpec((1, tq, r * HEAD_DIM), lambda bb, gg, i: (bb, i, gg)),
        out_shape=jax.ShapeDtypeStruct((b, s, N_NSA_HEADS * HEAD_DIM), BF16),
        scratch_shapes=[
            pltpu.VMEM((s, 2 * LANES), BF16),
            pltpu.VMEM((s, 2 * LANES), BF16),
            pltpu.VMEM((s, 2 * LANES), BF16),
            pltpu.VMEM((r * tq, 2 * LANES), BF16),
            pltpu.VMEM((r * tq, LANES), F32),
            pltpu.VMEM((r * tq, 2 * LANES), F32),
            pltpu.VMEM((r * tq, HEAD_DIM), F32),
            pltpu.VMEM((nsr, tq), F32),
        ],
        compiler_params=pltpu.CompilerParams(
            dimension_semantics=("parallel", "parallel", "arbitrary"), vmem_limit_bytes=VMEM_LIMIT),
        name="nsa",
    )(p, kc, vc, p, p, p, p, p, gains)


def _sb_kernel(q_ref, k_ref, v_ref, tri_ref, gain_ref, o_ref, acc_ref, c_ref, *, tk, parts):
    i = pl.program_id(2)
    acc_ref[...] = jnp.zeros(acc_ref.shape, F32)
    c_ref[...] = jnp.zeros(c_ref.shape, F32)
    tri = tri_ref[...]
    row = lax.broadcasted_iota(jnp.int32, (tk, tk), 0)
    col = lax.broadcasted_iota(jnp.int32, (tk, tk), 1)
    strict = col < row

    def tiles(k, v, work):
        rows = [pl.ds(part * tk, tk) for part, _ in work]
        zn = [_nt(q_ref[0, 0, r, :], k) for r in rows]
        lk = []
        for z, (_, masked) in zip(zn, work):
            neg_abs = pltpu.bitcast(pltpu.bitcast(z, jnp.uint32) | jnp.uint32(0x80000000), F32)
            x = jnp.minimum(z, 0.0) - jnp.log2(1.0 + jnp.exp2(neg_abs))
            lk.append(jnp.where(strict, x, 0.0) if masked else x)
        hi = [x.astype(BF16) for x in lk]
        lo = [(x - h.astype(F32)).astype(BF16) for x, h in zip(lk, hi)]
        incl = [_mm(jnp.concatenate([h, l], axis=1), tri) for h, l in zip(hi, lo)]
        a = []
        for n, z, r, (_, masked) in zip(incl, zn, rows, work):
            x = jnp.exp2(n + jnp.tile(c_ref[r, :], (1, tk // LANES)) - z)
            a.append((jnp.where(strict, x, 0.0) if masked else x).astype(BF16))
        for x, r in zip(a, rows):
            acc_ref[r, :] += _mm(x, v)
        for x, r in zip(lk, rows):
            c_ref[r, :] += jnp.sum(x, axis=-1, keepdims=True)

    def kv(j):
        k0 = pl.multiple_of(j * tk, tk)
        return k_ref[0, 0, pl.ds(k0, tk), :], v_ref[0, 0, pl.ds(k0, tk), :]

    for d in reversed(range(parts)):
        k, v = kv(parts * i + d)
        tiles(k, v, [(d, True)] + [(p, False) for p in range(d + 1, parts)])

    def body(jj, carry):
        k, v = kv(parts * i - 1 - jj)
        tiles(k, v, [(p, False) for p in range(parts)])
        return carry

    lax.fori_loop(0, parts * i, body, 0)
    o_ref[0] = _rms(acc_ref[...], gain_ref[0]).astype(BF16)


def _sb(p, tri, gains, tk, parts):
    b, _, s, _ = p.shape
    tq = parts * tk
    return pl.pallas_call(
        functools.partial(_sb_kernel, tk=tk, parts=parts),
        grid=(b, N_SB_HEADS, s // tq),
        in_specs=[
            pl.BlockSpec((1, 1, tq, LANES), lambda bb, h, i: (bb, BLK_QS + h, i, 0)),
            pl.BlockSpec((1, 1, s, LANES), lambda bb, h, i: (bb, BLK_KSB + h, 0, 0)),
            pl.BlockSpec((1, 1, s, LANES), lambda bb, h, i: (bb, BLK_VSB + h, 0, 0)),
            pl.BlockSpec((2 * tk, tk), lambda bb, h, i: (0, 0)),
            pl.BlockSpec((1, 1, HEAD_DIM), lambda bb, h, i: (h, 0, 0)),
        ],
        out_specs=pl.BlockSpec((1, tq, HEAD_DIM), lambda bb, h, i: (bb, i, h)),
        out_shape=jax.ShapeDtypeStruct((b, s, N_SB_HEADS * HEAD_DIM), BF16),
        scratch_shapes=[
            pltpu.VMEM((tq, HEAD_DIM), F32),
            pltpu.VMEM((tq, LANES), F32),
        ],
        compiler_params=pltpu.CompilerParams(
            dimension_semantics=("parallel", "parallel", "arbitrary"), vmem_limit_bytes=VMEM_LIMIT),
        name="sb",
    )(p, p, p, tri, gains)


def _out_kernel(x_ref, ya_ref, yb_ref, wa_ref, wb_ref, o_ref):
    o_ref[...] = x_ref[...] + _mm(ya_ref[...], wa_ref[...]) + _mm(yb_ref[...], wb_ref[...])


def _out_proj(x2d, ya, yb, wa, wb, tm):
    t, d = x2d.shape
    ka, kb = ya.shape[1], yb.shape[1]
    return pl.pallas_call(
        _out_kernel,
        grid=(t // tm,),
        in_specs=[
            pl.BlockSpec((tm, d), lambda i: (i, 0)),
            pl.BlockSpec((tm, ka), lambda i: (i, 0)),
            pl.BlockSpec((tm, kb), lambda i: (i, 0)),
            pl.BlockSpec((ka, d), lambda i: (0, 0)),
            pl.BlockSpec((kb, d), lambda i: (0, 0)),
        ],
        out_specs=pl.BlockSpec((tm, d), lambda i: (i, 0)),
        out_shape=jax.ShapeDtypeStruct((t, d), F32),
        compiler_params=pltpu.CompilerParams(
            dimension_semantics=("parallel",), vmem_limit_bytes=VMEM_LIMIT),
        name="out_proj",
    )(x2d, ya, yb, wa, wb)


def _tile(n, pref):
    t = min(n, pref)
    while n % t:
        t //= 2
    return t


def _pack_w_in(w_in):
    d = w_in.shape[0]
    hd = HEAD_DIM
    kv = N_NSA_KV * hd
    nq = N_NSA_HEADS * hd
    o_gate = nq + 6 * kv
    o_sb = o_gate + N_NSA_HEADS * 3
    gate = w_in[:, o_gate:o_sb].reshape(d, N_NSA_KV, NSA_GROUP * 3)
    gate = jnp.pad(gate, ((0, 0), (0, 0), (0, hd - NSA_GROUP * 3))).reshape(d, N_NSA_KV * hd)
    pad = jnp.zeros((d, (BLK_QS - BLK_GATE - N_NSA_KV) * hd), w_in.dtype)
    return jnp.concatenate([w_in[:, :o_gate], gate, pad, w_in[:, o_sb:]], axis=1).astype(BF16)


def kernel(x, positions, ffn1_norm, ffn1_w_gate, ffn1_w_up, ffn1_w_down, mix_norm, w_in,
           nsa_q_norm, nsa_k_cmp_norm, nsa_k_slc_norm, nsa_k_win_norm,
           cmp_k_pos, cmp_k_w1, cmp_k_w2, cmp_v_pos, cmp_v_w1, cmp_v_w2,
           nsa_out_norm, sb_out_norm, w_out, ffn2_norm, ffn2_w_gate, ffn2_w_up, ffn2_w_down):
    b, s, d = x.shape
    depth = w_in.shape[0]
    hd = HEAD_DIM
    hid = cmp_k_w2.shape[1]
    t = b * s
    tm = _tile(t, 512)
    tf = _tile(ffn1_w_gate.shape[2], 512)
    tq_nsa = 256
    tk_sb, parts_sb = 256, 4
    assert s % (tk_sb * parts_sb) == 0 and s // SLC_BLOCK <= LANES and WINDOW % tq_nsa == 0 and s % tq_nsa == 0

    inv_freq = ROPE_THETA ** (-jnp.arange(0, hd, 2, dtype=F32) / hd)
    ang = positions.astype(F32)[..., None] * inv_freq
    cosf = jnp.concatenate([jnp.cos(ang), jnp.cos(ang)], axis=-1)
    sinf = jnp.concatenate([-jnp.sin(ang), jnp.sin(ang)], axis=-1)

    idx = jnp.arange(tk_sb, dtype=jnp.int32)
    tri = (idx[:, None] >= idx[None, :]).astype(BF16)
    tri = jnp.concatenate([tri, tri], axis=0)

    def w1cat(w1):
        w = w1.reshape(CMP_BLOCK, hd, hid)
        return jnp.concatenate([w[:CMP_STRIDE], w[CMP_STRIDE:]], axis=-1).astype(BF16)

    for l in range(depth):
        x2 = _ffn(x.reshape(t, d), ffn1_norm[l], ffn1_w_gate[l].astype(BF16), ffn1_w_up[l].astype(BF16),
                  ffn1_w_down[l].astype(BF16), tm, tf)

        head_gains = jnp.zeros((8, hd), F32)
        head_gains = head_gains.at[0].set(nsa_q_norm[l]).at[1].set(nsa_k_slc_norm[l]).at[2].set(nsa_k_win_norm[l])
        p, csrc = _in_proj(x2.reshape(b, s, d), mix_norm[l], _pack_w_in(w_in[l]), cosf, sinf, head_gains, tm=_tile(s, 512))

        kc = _compress(csrc, 0, cmp_k_pos[l], w1cat(cmp_k_w1[l]), cmp_k_w2[l].astype(BF16),
                       nsa_k_cmp_norm[l], True, F32)
        vc = _compress(csrc, N_NSA_KV, cmp_v_pos[l], w1cat(cmp_v_w1[l]), cmp_v_w2[l].astype(BF16),
                       jnp.ones((hd,), F32), False, BF16)

        y_nsa = _nsa(p, kc, vc, nsa_out_norm[l].reshape(N_NSA_KV, NSA_GROUP, hd), tq_nsa)
        y_sb = _sb(p, tri, sb_out_norm[l].reshape(N_SB_HEADS, 1, hd), tk_sb, parts_sb)

        nmix = N_NSA_HEADS * hd
        x3 = _out_proj(x2, y_nsa.reshape(t, nmix), y_sb.reshape(t, N_SB_HEADS * hd),
                       w_out[l, :nmix].astype(BF16), w_out[l, nmix:].astype(BF16), tm)

        x = _ffn(x3, ffn2_norm[l], ffn2_w_gate[l].astype(BF16), ffn2_w_up[l].astype(BF16),
                 ffn2_w_down[l].astype(BF16), tm, tf).reshape(b, s, d)
    return x
```

```python
import functools

import jax
import jax.numpy as jnp
from jax import lax
from jax.experimental import pallas as pl
from jax.experimental.pallas import tpu as pltpu

HEAD_DIM = 128
N_NSA_HEADS = 8
N_NSA_KV = 2
NSA_GROUP = N_NSA_HEADS // N_NSA_KV
N_SB_HEADS = 8
CMP_BLOCK = 32
CMP_STRIDE = 16
SLC_BLOCK = 64
N_SELECT = 16
WINDOW = 512
ROPE_THETA = 10000.0
EPS = 1e-6
NEG_INF = -1e30
FORCE_SCORE = 1e9
LOG2E = 1.4426950408889634
MASK_SCORE = 1e30

LANES = 128
VMEM_LIMIT = 56 * 1024 * 1024

BLK_Q = 0
BLK_KC, BLK_VC = 8, 10
BLK_KS, BLK_VS = 12, 14
BLK_KW, BLK_VW = 16, 18
BLK_GATE = 20
BLK_QS, BLK_KSB, BLK_VSB = 24, 32, 40
N_BLK = 48
PROJ_NB = 4

BF16 = jnp.bfloat16
F32 = jnp.float32


def _nt(a, b):
    return lax.dot_general(a, b, (((1,), (1,)), ((), ())), preferred_element_type=F32)


def _mm(a, b):
    return jnp.dot(a, b, preferred_element_type=F32)


def _rms(v, g):
    return v * lax.rsqrt(jnp.mean(v * v, axis=-1, keepdims=True) + EPS) * g


def _ffn_kernel(x_ref, g_ref, wg_ref, wu_ref, wd_ref, o_ref, xn_ref):
    j = pl.program_id(1)

    @pl.when(j == 0)
    def _():
        xn_ref[...] = _rms(x_ref[...], g_ref[...]).astype(BF16)
        o_ref[...] = jnp.zeros_like(o_ref)

    xn = xn_ref[...]
    h = _mm(xn, wg_ref[...])
    u = _mm(xn, wu_ref[...])
    a = (h * (1.0 / (1.0 + jnp.exp(-h)))) * u
    o_ref[...] += _mm(a.astype(BF16), wd_ref[...])

    @pl.when(j == pl.num_programs(1) - 1)
    def _():
        o_ref[...] = x_ref[...] + 0.5 * o_ref[...]


def _ffn(x2d, gain, wg, wu, wd, tm, tf):
    t, d = x2d.shape
    ff = wg.shape[1]
    return pl.pallas_call(
        _ffn_kernel,
        grid=(t // tm, ff // tf),
        in_specs=[
            pl.BlockSpec((tm, d), lambda i, j: (i, 0)),
            pl.BlockSpec((1, d), lambda i, j: (0, 0)),
            pl.BlockSpec((d, tf), lambda i, j: (0, j)),
            pl.BlockSpec((d, tf), lambda i, j: (0, j)),
            pl.BlockSpec((tf, d), lambda i, j: (j, 0)),
        ],
        out_specs=pl.BlockSpec((tm, d), lambda i, j: (i, 0)),
        out_shape=jax.ShapeDtypeStruct((t, d), F32),
        scratch_shapes=[pltpu.VMEM((tm, d), BF16)],
        compiler_params=pltpu.CompilerParams(
            dimension_semantics=("parallel", "arbitrary"), vmem_limit_bytes=VMEM_LIMIT),
        name="ffn",
    )(x2d, gain.reshape(1, d), wg, wu, wd)


def _proj_kernel(x_ref, g_ref, w_ref, cos_ref, sin_ref, hg_ref, p_ref, c_ref, xn_ref):
    j = pl.program_id(2)
    scale = HEAD_DIM ** -0.5

    @pl.when(j == 0)
    def _():
        xn_ref[...] = _rms(x_ref[0], g_ref[...]).astype(BF16)

    acc = _mm(xn_ref[...], w_ref[...])

    def blk(c):
        return acc[:, c * LANES:(c + 1) * LANES]

    def rope(v):
        return v * cos_ref[0] + pltpu.roll(v, HEAD_DIM // 2, 1) * sin_ref[0]

    def put(c, v):
        p_ref[0, c] = v.astype(BF16)

    @pl.when(j < 2)
    def _():
        for c in range(PROJ_NB):
            put(c, rope(_rms(blk(c), hg_ref[0:1, :])) * (scale * LOG2E))

    @pl.when(j == 2)
    def _():
        for c in range(2):
            v = rope(blk(c))
            put(c, v)
            c_ref[0, c] = v
        for c in range(2, 4):
            put(c, blk(c))
            c_ref[0, c] = blk(c)

    @pl.when(j == 3)
    def _():
        for c in range(2):
            put(c, rope(_rms(blk(c), hg_ref[1:2, :])))
        for c in range(2, 4):
            put(c, blk(c))

    @pl.when(j == 4)
    def _():
        for c in range(2):
            put(c, rope(_rms(blk(c), hg_ref[2:3, :])))
        for c in range(2, 4):
            put(c, blk(c))

    @pl.when(j == 5)
    def _():
        for c in range(PROJ_NB):
            put(c, blk(c))

    @pl.when((j == 6) | (j == 7))
    def _():
        for c in range(PROJ_NB):
            put(c, blk(c) * (-scale * LOG2E))

    @pl.when(j > 7)
    def _():
        for c in range(PROJ_NB):
            put(c, blk(c))


def _in_proj(x, gain, w, cosf, sinf, head_gains, tm):
    b, s, d = x.shape
    tn = PROJ_NB * LANES
    return pl.pallas_call(
        _proj_kernel,
        grid=(b, s // tm, N_BLK // PROJ_NB),
        in_specs=[
            pl.BlockSpec((1, tm, d), lambda bb, i, j: (bb, i, 0)),
            pl.BlockSpec((1, d), lambda bb, i, j: (0, 0)),
            pl.BlockSpec((d, tn), lambda bb, i, j: (0, j)),
            pl.BlockSpec((1, tm, LANES), lambda bb, i, j: (bb, i, 0)),
            pl.BlockSpec((1, tm, LANES), lambda bb, i, j: (bb, i, 0)),
            pl.BlockSpec((8, LANES), lambda bb, i, j: (0, 0)),
        ],
        out_specs=[
            pl.BlockSpec((1, PROJ_NB, tm, LANES), lambda bb, i, j: (bb, j, i, 0)),
            pl.BlockSpec((1, 4, tm, LANES), lambda bb, i, j: (bb, 0, i, 0)),
        ],
        out_shape=[
            jax.ShapeDtypeStruct((b, N_BLK, s, LANES), BF16),
            jax.ShapeDtypeStruct((b, 4, s, LANES), F32),
        ],
        scratch_shapes=[pltpu.VMEM((tm, d), BF16)],
        compiler_params=pltpu.CompilerParams(
            dimension_semantics=("parallel", "parallel", "arbitrary"), vmem_limit_bytes=VMEM_LIMIT),
        name="in_proj",
    )(x, gain.reshape(1, d), w, cosf, sinf, head_gains)


def _compress_kernel(x_ref, pos_ref, w1_ref, w2_ref, g_ref, o_ref, *, normalize):
    nb = o_ref.shape[2]
    hid = w2_ref.shape[0]
    acc = jnp.zeros((nb, 2 * hid), F32)
    bias = jnp.zeros((8, hid), F32)
    for l in range(CMP_STRIDE):
        rows = x_ref[0, 0, pl.ds(l, nb, stride=CMP_STRIDE), :]
        acc = acc + _mm(rows.astype(BF16), w1_ref[l])
        p_lo = jnp.broadcast_to(pos_ref[l:l + 1, :], (8, HEAD_DIM)).astype(BF16)
        p_hi = jnp.broadcast_to(pos_ref[l + CMP_STRIDE:l + CMP_STRIDE + 1, :], (8, HEAD_DIM)).astype(BF16)
        bias = bias + _mm(p_lo, w1_ref[l, :, :hid]) + _mm(p_hi, w1_ref[l, :, hid:])
    hidden = acc[:, :hid] + pltpu.roll(acc[:, hid:], nb - 1, 0) + bias[0:1, :]
    act = 0.5 * hidden * (1.0 + jnp.tanh(0.7978845608028654 * (hidden + 0.044715 * hidden * hidden * hidden)))
    out = _mm(act.astype(BF16), w2_ref[...])
    if normalize:
        out = _rms(out, g_ref[...])
    row = lax.broadcasted_iota(jnp.int32, out.shape, 0)
    o_ref[0, 0] = jnp.where(row < nb - 1, out, 0.0).astype(o_ref.dtype)


def _compress(src, first_blk, pos, w1cat, w2, gain, normalize, out_dtype):
    b, _, s, _ = src.shape
    nb = s // CMP_STRIDE
    hid = w2.shape[0]
    return pl.pallas_call(
        functools.partial(_compress_kernel, normalize=normalize),
        grid=(b, N_NSA_KV),
        in_specs=[
            pl.BlockSpec((1, 1, s, LANES), lambda bb, g: (bb, first_blk + g, 0, 0)),
            pl.BlockSpec((CMP_BLOCK, HEAD_DIM), lambda bb, g: (0, 0)),
            pl.BlockSpec((CMP_STRIDE, HEAD_DIM, 2 * hid), lambda bb, g: (0, 0, 0)),
            pl.BlockSpec((hid, HEAD_DIM), lambda bb, g: (0, 0)),
            pl.BlockSpec((1, HEAD_DIM), lambda bb, g: (0, 0)),
        ],
        out_specs=pl.BlockSpec((1, 1, nb, HEAD_DIM), lambda bb, g: (bb, g, 0, 0)),
        out_shape=jax.ShapeDtypeStruct((b, N_NSA_KV, nb, HEAD_DIM), out_dtype),
        compiler_params=pltpu.CompilerParams(
            dimension_semantics=("parallel", "parallel"), vmem_limit_bytes=VMEM_LIMIT),
        name="compress_k" if normalize else "compress_v",
    )(src, pos, w1cat, w2, gain.reshape(1, HEAD_DIM))


def _nsa_kernel(q_ref, kc_ref, vc_ref, ks_ref, vs_ref, kw_ref, vw_ref, gate_ref, gain_ref,
                o_ref, kaug_ref, vaug_ref, vwaug_ref, qaug_ref, m_ref, acc_ref, mix_ref, sc_ref, *, tq):
    i = pl.program_id(2)
    r = NSA_GROUP
    q0 = i * tq
    s_len = ks_ref.shape[2]
    ncb = kc_ref.shape[2]
    n_slc = s_len // SLC_BLOCK
    nsr = -(-n_slc // 8) * 8
    heads = range(r)

    @pl.when(i == 0)
    def _():
        lane = lax.broadcasted_iota(jnp.int32, (s_len, LANES), 1)
        kblk = lax.broadcasted_iota(jnp.int32, (s_len, LANES), 0) // SLC_BLOCK
        kaug_ref[:, :HEAD_DIM] = ks_ref[0, 0]
        kaug_ref[:, HEAD_DIM:] = jnp.where(lane == kblk, MASK_SCORE, 0.0).astype(BF16)
        ones = jnp.ones((s_len, LANES), BF16)
        vaug_ref[:, :HEAD_DIM] = vs_ref[0, 0]
        vaug_ref[:, HEAD_DIM:] = ones
        vwaug_ref[:, :HEAD_DIM] = vw_ref[0, 0]
        vwaug_ref[:, HEAD_DIM:] = ones

    row = lax.broadcasted_iota(jnp.int32, (tq, tq), 0)
    col = lax.broadcasted_iota(jnp.int32, (tq, tq), 1)
    causal = col <= row
    gt = 1.0 / (1.0 + jnp.exp(-gate_ref[0, 0].astype(F32)))

    def gate(h, c):
        return gt[:, 3 * h + c:3 * h + c + 1]

    kc = kc_ref[0, 0].astype(BF16)
    vc = vc_ref[0, 0]
    cend = lax.broadcasted_iota(jnp.int32, (tq, ncb), 1) * CMP_STRIDE + (CMP_BLOCK - 1)
    cvalid = cend <= q0 + lax.broadcasted_iota(jnp.int32, (tq, ncb), 0)
    imp = jnp.zeros((tq, ncb), F32)
    for h in heads:
        s = jnp.where(cvalid, _nt(q_ref[0, h], kc), NEG_INF)
        e = jnp.where(cvalid, jnp.exp2(s - jnp.max(s, axis=-1, keepdims=True)), 0.0)
        den = jnp.sum(e, axis=-1, keepdims=True)
        pc = e * (1.0 / jnp.where(den > 0.0, den, 1.0))
        imp = imp + pc
        mix_ref[h * tq:(h + 1) * tq, :] = gate(h, 0) * _mm(pc.astype(BF16), vc)

    nsb = nsr
    imp_hi = imp.astype(BF16)
    imp_lo = (imp - imp_hi.astype(F32)).astype(BF16)
    brow = lax.broadcasted_iota(jnp.int32, (nsb, ncb), 0)
    ccol = lax.broadcasted_iota(jnp.int32, (nsb, ncb), 1)
    ratio = SLC_BLOCK // CMP_STRIDE
    span = CMP_BLOCK // CMP_STRIDE
    cover = ((ccol >= ratio * brow - (span - 1)) & (ccol <= ratio * brow + ratio - 1)
             & (ccol < ncb - 1)).astype(BF16)
    p_slc = _nt(cover, imp_hi) + _nt(cover, imp_lo)
    bidx = lax.broadcasted_iota(jnp.int32, (nsb, tq), 0)
    t_lane = q0 + lax.broadcasted_iota(jnp.int32, (nsb, tq), 1)
    cur = t_lane // SLC_BLOCK
    forced = (bidx == 0) | (bidx == cur) | (bidx == cur - 1)
    score = jnp.where(bidx <= cur, jnp.where(forced, FORCE_SCORE, p_slc), NEG_INF)
    score = jnp.where(bidx < n_slc, score, -3e38)
    sc_ref[...] = score
    ranks = []
    for c0 in range(0, nsb, 8):
        mine = score[c0:c0 + 8, :]
        brow8 = c0 + lax.broadcasted_iota(jnp.int32, (8, tq), 0)
        rank = jnp.zeros((8, tq), F32)
        for bp in range(n_slc):
            other = sc_ref[bp:bp + 1, :]
            ge = jnp.where(other >= mine, 1.0, 0.0)
            gt_ = jnp.where(other > mine, 1.0, 0.0)
            if bp < c0:
                rank = rank + ge
            elif bp >= c0 + 8:
                rank = rank + gt_
            else:
                rank = rank + jnp.where(brow8 > bp, ge, gt_)
        ranks.append(rank)
    rank = jnp.concatenate(ranks, axis=0)
    sel_t = jnp.where((rank < float(min(N_SELECT, n_slc))) & (bidx <= cur) & (bidx < n_slc), 0.0, -1.0)
    if nsb < LANES:
        sel_t = jnp.concatenate([sel_t, jnp.full((LANES - nsb, tq), -1.0, F32)], axis=0)
    selm1 = jnp.transpose(sel_t).astype(BF16)
    for h in heads:
        qaug_ref[h * tq:(h + 1) * tq, :HEAD_DIM] = q_ref[0, h]
        qaug_ref[h * tq:(h + 1) * tq, HEAD_DIM:] = selm1

    def init():
        m_ref[...] = jnp.full(m_ref.shape, NEG_INF, F32)
        acc_ref[...] = jnp.zeros(acc_ref.shape, F32)

    def flash(q_of, k, v, mask):
        width = k.shape[0]
        s = [_nt(q_of(h), k) for h in heads]
        p, alpha = [], []
        for h in heads:
            rows = pl.ds(h * tq, tq)
            sh = s[h] if mask is None else jnp.where(mask, s[h], NEG_INF)
            m_prev = m_ref[rows, :]
            m_next = jnp.maximum(m_prev, jnp.max(sh, axis=-1, keepdims=True))
            p.append(jnp.exp2(sh - jnp.tile(m_next, (1, width // LANES))).astype(BF16))
            alpha.append(jnp.exp2(m_prev - m_next))
            m_ref[rows, :] = m_next
        for h in heads:
            rows = pl.ds(h * tq, tq)
            acc_ref[rows, :] = jnp.tile(alpha[h], (1, 2)) * acc_ref[rows, :] + _mm(p[h], v)

    def result(h):
        rows = pl.ds(h * tq, tq)
        return acc_ref[rows, :HEAD_DIM] / acc_ref[rows, HEAD_DIM:]

    def kv_rows(j):
        return pl.ds(pl.multiple_of(j * tq, tq), tq)

    init()

    def q_slc(h):
        return qaug_ref[h * tq:(h + 1) * tq, :]

    def slc_body(j, carry):
        rows = pl.ds(pl.multiple_of(j * (2 * tq), 2 * tq), 2 * tq)
        flash(q_slc, kaug_ref[rows, :], vaug_ref[rows, :], None)
        return carry

    lax.fori_loop(0, i // 2, slc_body, 0)

    @pl.when(i % 2 == 1)
    def _():
        flash(q_slc, kaug_ref[kv_rows(i - 1), :], vaug_ref[kv_rows(i - 1), :], None)

    flash(q_slc, kaug_ref[kv_rows(i), :], vaug_ref[kv_rows(i), :], causal)
    for h in heads:
        mix_ref[h * tq:(h + 1) * tq, :] += gate(h, 1) * result(h)

    nback = WINDOW // tq

    def window(back):
        width = (back + 1) * tq
        rows = pl.ds(pl.multiple_of((i - back) * tq, tq), width)
        k = kw_ref[0, 0, rows, :]
        v = vwaug_ref[rows, :]
        ahead = (lax.broadcasted_iota(jnp.int32, (tq, width), 1)
                 - lax.broadcasted_iota(jnp.int32, (tq, width), 0)) - back * tq
        mask = (ahead <= 0) & (ahead > -WINDOW)
        s = [jnp.where(mask, _nt(q_ref[0, h], k), NEG_INF) for h in heads]
        p = [jnp.exp2(sh - jnp.max(sh, axis=-1, keepdims=True)).astype(BF16) for sh in s]
        o = [_mm(ph, v) for ph in p]
        for h in heads:
            merged = mix_ref[h * tq:(h + 1) * tq, :] + gate(h, 2) * (o[h][:, :HEAD_DIM] / o[h][:, HEAD_DIM:])
            o_ref[0, :, h * HEAD_DIM:(h + 1) * HEAD_DIM] = _rms(merged, gain_ref[0, h:h + 1, :]).astype(BF16)

    for back in range(nback):
        @pl.when(i == back)
        def _(back=back):
            window(back)

    @pl.when(i >= nback)
    def _():
        window(nback)


def _nsa(p, kc, vc, gains, tq):
    b, _, s, _ = p.shape
    g, r = N_NSA_KV, NSA_GROUP
    ncb = kc.shape[2]
    nsr = -(-(s // SLC_BLOCK) // 8) * 8

    def kv_spec(first):
        return pl.BlockSpec((1, 1, s, LANES), lambda bb, gg, i: (bb, first + gg, 0, 0))

    return pl.pallas_call(
        functools.partial(_nsa_kernel, tq=tq),
        grid=(b, g, s // tq),
        in_specs=[
            pl.BlockSpec((1, r, tq, LANES), lambda bb, gg, i: (bb, gg, i, 0)),
            pl.BlockSpec((1, 1, ncb, LANES), lambda bb, gg, i: (bb, gg, 0, 0)),
            pl.BlockSpec((1, 1, ncb, LANES), lambda bb, gg, i: (bb, gg, 0, 0)),
            kv_spec(BLK_KS), kv_spec(BLK_VS), kv_spec(BLK_KW), kv_spec(BLK_VW),
            pl.BlockSpec((1, 1, tq, LANES), lambda bb, gg, i: (bb, BLK_GATE + gg, i, 0)),
            pl.BlockSpec((1, r, HEAD_DIM), lambda bb, gg, i: (gg, 0, 0)),
        ],
        out_specs=pl.BlockSpec((1, tq, r * HEAD_DIM), lambda bb, gg, i: (bb, i, gg)),
        out_shape=jax.ShapeDtypeStruct((b, s, N_NSA_HEADS * HEAD_DIM), BF16),
        scratch_shapes=[
            pltpu.VMEM((s, 2 * LANES), BF16),
            pltpu.VMEM((s, 2 * LANES), BF16),
            pltpu.VMEM((s, 2 * LANES), BF16),
            pltpu.VMEM((r * tq, 2 * LANES), BF16),
            pltpu.VMEM((r * tq, LANES), F32),
            pltpu.VMEM((r * tq, 2 * LANES), F32),
            pltpu.VMEM((r * tq, HEAD_DIM), F32),
            pltpu.VMEM((nsr, tq), F32),
        ],
        compiler_params=pltpu.CompilerParams(
            dimension_semantics=("parallel", "parallel", "arbitrary"), vmem_limit_bytes=VMEM_LIMIT),
        name="nsa",
    )(p, kc, vc, p, p, p, p, p, gains)


def _sb_kernel(q_ref, k_ref, v_ref, tri_ref, gain_ref, o_ref, acc_ref, c_ref, *, tk, parts):
    i = pl.program_id(2)
    acc_ref[...] = jnp.zeros(acc_ref.shape, F32)
    c_ref[...] = jnp.zeros(c_ref.shape, F32)
    tri = tri_ref[...]
    row = lax.broadcasted_iota(jnp.int32, (tk, tk), 0)
    col = lax.broadcasted_iota(jnp.int32, (tk, tk), 1)
    strict = col < row

    def tiles(k, v, work):
        rows = [pl.ds(part * tk, tk) for part, _ in work]
        zn = [_nt(q_ref[0, 0, r, :], k) for r in rows]
        lk = []
        for z, (_, masked) in zip(zn, work):
            x = jnp.minimum(z, 0.0) - jnp.log2(1.0 + jnp.exp2(-jnp.abs(z)))
            lk.append(jnp.where(strict, x, 0.0) if masked else x)
        hi = [x.astype(BF16) for x in lk]
        lo = [(x - h.astype(F32)).astype(BF16) for x, h in zip(lk, hi)]
        incl = [_mm(jnp.concatenate([h, l], axis=1), tri) for h, l in zip(hi, lo)]
        a = []
        for n, z, r, (_, masked) in zip(incl, zn, rows, work):
            x = jnp.exp2(n + jnp.tile(c_ref[r, :], (1, tk // LANES)) - z)
            a.append((jnp.where(strict, x, 0.0) if masked else x).astype(BF16))
        for x, r in zip(a, rows):
            acc_ref[r, :] += _mm(x, v)
        for x, r in zip(lk, rows):
            c_ref[r, :] += jnp.sum(x, axis=-1, keepdims=True)

    def kv(j):
        k0 = pl.multiple_of(j * tk, tk)
        return k_ref[0, 0, pl.ds(k0, tk), :], v_ref[0, 0, pl.ds(k0, tk), :]

    for d in reversed(range(parts)):
        k, v = kv(parts * i + d)
        tiles(k, v, [(d, True)] + [(p, False) for p in range(d + 1, parts)])

    def body(jj, carry):
        k, v = kv(parts * i - 1 - jj)
        tiles(k, v, [(p, False) for p in range(parts)])
        return carry

    lax.fori_loop(0, parts * i, body, 0)
    o_ref[0] = _rms(acc_ref[...], gain_ref[0]).astype(BF16)


def _sb(p, tri, gains, tk, parts):
    b, _, s, _ = p.shape
    tq = parts * tk
    return pl.pallas_call(
        functools.partial(_sb_kernel, tk=tk, parts=parts),
        grid=(b, N_SB_HEADS, s // tq),
        in_specs=[
            pl.BlockSpec((1, 1, tq, LANES), lambda bb, h, i: (bb, BLK_QS + h, i, 0)),
            pl.BlockSpec((1, 1, s, LANES), lambda bb, h, i: (bb, BLK_KSB + h, 0, 0)),
            pl.BlockSpec((1, 1, s, LANES), lambda bb, h, i: (bb, BLK_VSB + h, 0, 0)),
            pl.BlockSpec((2 * tk, tk), lambda bb, h, i: (0, 0)),
            pl.BlockSpec((1, 1, HEAD_DIM), lambda bb, h, i: (h, 0, 0)),
        ],
        out_specs=pl.BlockSpec((1, tq, HEAD_DIM), lambda bb, h, i: (bb, i, h)),
        out_shape=jax.ShapeDtypeStruct((b, s, N_SB_HEADS * HEAD_DIM), BF16),
        scratch_shapes=[
            pltpu.VMEM((tq, HEAD_DIM), F32),
            pltpu.VMEM((tq, LANES), F32),
        ],
        compiler_params=pltpu.CompilerParams(
            dimension_semantics=("parallel", "parallel", "arbitrary"), vmem_limit_bytes=VMEM_LIMIT),
        name="sb",
    )(p, p, p, tri, gains)


def _out_kernel(x_ref, ya_ref, yb_ref, wa_ref, wb_ref, o_ref):
    o_ref[...] = x_ref[...] + _mm(ya_ref[...], wa_ref[...]) + _mm(yb_ref[...], wb_ref[...])


def _out_proj(x2d, ya, yb, wa, wb, tm):
    t, d = x2d.shape
    ka, kb = ya.shape[1], yb.shape[1]
    return pl.pallas_call(
        _out_kernel,
        grid=(t // tm,),
        in_specs=[
            pl.BlockSpec((tm, d), lambda i: (i, 0)),
            pl.BlockSpec((tm, ka), lambda i: (i, 0)),
            pl.BlockSpec((tm, kb), lambda i: (i, 0)),
            pl.BlockSpec((ka, d), lambda i: (0, 0)),
            pl.BlockSpec((kb, d), lambda i: (0, 0)),
        ],
        out_specs=pl.BlockSpec((tm, d), lambda i: (i, 0)),
        out_shape=jax.ShapeDtypeStruct((t, d), F32),
        compiler_params=pltpu.CompilerParams(
            dimension_semantics=("parallel",), vmem_limit_bytes=VMEM_LIMIT),
        name="out_proj",
    )(x2d, ya, yb, wa, wb)


def _tile(n, pref):
    t = min(n, pref)
    while n % t:
        t //= 2
    return t


def _pack_w_in(w_in):
    d = w_in.shape[0]
    hd = HEAD_DIM
    kv = N_NSA_KV * hd
    nq = N_NSA_HEADS * hd
    o_gate = nq + 6 * kv
    o_sb = o_gate + N_NSA_HEADS * 3
    gate = w_in[:, o_gate:o_sb].reshape(d, N_NSA_KV, NSA_GROUP * 3)
    gate = jnp.pad(gate, ((0, 0), (0, 0), (0, hd - NSA_GROUP * 3))).reshape(d, N_NSA_KV * hd)
    pad = jnp.zeros((d, (BLK_QS - BLK_GATE - N_NSA_KV) * hd), w_in.dtype)
    return jnp.concatenate([w_in[:, :o_gate], gate, pad, w_in[:, o_sb:]], axis=1).astype(BF16)


def kernel(x, positions, ffn1_norm, ffn1_w_gate, ffn1_w_up, ffn1_w_down, mix_norm, w_in,
           nsa_q_norm, nsa_k_cmp_norm, nsa_k_slc_norm, nsa_k_win_norm,
           cmp_k_pos, cmp_k_w1, cmp_k_w2, cmp_v_pos, cmp_v_w1, cmp_v_w2,
           nsa_out_norm, sb_out_norm, w_out, ffn2_norm, ffn2_w_gate, ffn2_w_up, ffn2_w_down):
    b, s, d = x.shape
    depth = w_in.shape[0]
    hd = HEAD_DIM
    hid = cmp_k_w2.shape[1]
    t = b * s
    tm = _tile(t, 512)
    tf = _tile(ffn1_w_gate.shape[2], 512)
    tq_nsa = 256
    tk_sb, parts_sb = 256, 4
    assert s % (tk_sb * parts_sb) == 0 and s // SLC_BLOCK <= LANES and WINDOW % tq_nsa == 0 and s % tq_nsa == 0

    inv_freq = ROPE_THETA ** (-jnp.arange(0, hd, 2, dtype=F32) / hd)
    ang = positions.astype(F32)[..., None] * inv_freq
    cosf = jnp.concatenate([jnp.cos(ang), jnp.cos(ang)], axis=-1)
    sinf = jnp.concatenate([-jnp.sin(ang), jnp.sin(ang)], axis=-1)

    idx = jnp.arange(tk_sb, dtype=jnp.int32)
    tri = (idx[:, None] >= idx[None, :]).astype(BF16)
    tri = jnp.concatenate([tri, tri], axis=0)

    def w1cat(w1):
        w = w1.reshape(CMP_BLOCK, hd, hid)
        return jnp.concatenate([w[:CMP_STRIDE], w[CMP_STRIDE:]], axis=-1).astype(BF16)

    for l in range(depth):
        x2 = _ffn(x.reshape(t, d), ffn1_norm[l], ffn1_w_gate[l].astype(BF16), ffn1_w_up[l].astype(BF16),
                  ffn1_w_down[l].astype(BF16), tm, tf)

        head_gains = jnp.zeros((8, hd), F32)
        head_gains = head_gains.at[0].set(nsa_q_norm[l]).at[1].set(nsa_k_slc_norm[l]).at[2].set(nsa_k_win_norm[l])
        p, csrc = _in_proj(x2.reshape(b, s, d), mix_norm[l], _pack_w_in(w_in[l]), cosf, sinf, head_gains, tm=_tile(s, 1024))

        kc = _compress(csrc, 0, cmp_k_pos[l], w1cat(cmp_k_w1[l]), cmp_k_w2[l].astype(BF16),
                       nsa_k_cmp_norm[l], True, F32)
        vc = _compress(csrc, N_NSA_KV, cmp_v_pos[l], w1cat(cmp_v_w1[l]), cmp_v_w2[l].astype(BF16),
                       jnp.ones((hd,), F32), False, BF16)

        y_nsa = _nsa(p, kc, vc, nsa_out_norm[l].reshape(N_NSA_KV, NSA_GROUP, hd), tq_nsa)
        y_sb = _sb(p, tri, sb_out_norm[l].reshape(N_SB_HEADS, 1, hd), tk_sb, parts_sb)

        nmix = N_NSA_HEADS * hd
        x3 = _out_proj(x2, y_nsa.reshape(t, nmix), y_sb.reshape(t, N_SB_HEADS * hd),
                       w_out[l, :nmix].astype(BF16), w_out[l, nmix:].astype(BF16), tm)

        x = _ffn(x3, ffn2_norm[l], ffn2_w_gate[l].astype(BF16), ffn2_w_up[l].astype(BF16),
                 ffn2_w_down[l].astype(BF16), tm, tf).reshape(b, s, d)
    return x
```

```python
import functools

import jax
import jax.numpy as jnp
from jax import lax
from jax.experimental import pallas as pl
from jax.experimental.pallas import tpu as pltpu

HEAD_DIM = 128
N_NSA_HEADS = 8
N_NSA_KV = 2
NSA_GROUP = N_NSA_HEADS // N_NSA_KV
N_SB_HEADS = 8
CMP_BLOCK = 32
CMP_STRIDE = 16
SLC_BLOCK = 64
N_SELECT = 16
WINDOW = 512
ROPE_THETA = 10000.0
EPS = 1e-6
NEG_INF = -1e30
FORCE_SCORE = 1e9
LOG2E = 1.4426950408889634
MASK_SCORE = 1e30

LANES = 128
VMEM_LIMIT = 56 * 1024 * 1024

BLK_Q = 0
BLK_KC, BLK_VC = 8, 10
BLK_KS, BLK_VS = 12, 14
BLK_KW, BLK_VW = 16, 18
BLK_GATE = 20
BLK_QS, BLK_KSB, BLK_VSB = 24, 32, 40
N_BLK = 48
PROJ_NB = 8

BF16 = jnp.bfloat16
F32 = jnp.float32


def _nt(a, b):
    return lax.dot_general(a, b, (((1,), (1,)), ((), ())), preferred_element_type=F32)


def _mm(a, b):
    return jnp.dot(a, b, preferred_element_type=F32)


def _rms(v, g):
    return v * lax.rsqrt(jnp.mean(v * v, axis=-1, keepdims=True) + EPS) * g


def _ffn_kernel(x_ref, g_ref, wg_ref, wu_ref, wd_ref, o_ref, xn_ref):
    j = pl.program_id(1)

    @pl.when(j == 0)
    def _():
        xn_ref[...] = _rms(x_ref[...], g_ref[...]).astype(BF16)
        o_ref[...] = jnp.zeros_like(o_ref)

    xn = xn_ref[...]
    h = _mm(xn, wg_ref[...])
    u = _mm(xn, wu_ref[...])
    a = (h * (1.0 / (1.0 + jnp.exp(-h)))) * u
    o_ref[...] += _mm(a.astype(BF16), wd_ref[...])

    @pl.when(j == pl.num_programs(1) - 1)
    def _():
        o_ref[...] = x_ref[...] + 0.5 * o_ref[...]


def _ffn(x2d, gain, wg, wu, wd, tm, tf):
    t, d = x2d.shape
    ff = wg.shape[1]
    return pl.pallas_call(
        _ffn_kernel,
        grid=(t // tm, ff // tf),
        in_specs=[
            pl.BlockSpec((tm, d), lambda i, j: (i, 0)),
            pl.BlockSpec((1, d), lambda i, j: (0, 0)),
            pl.BlockSpec((d, tf), lambda i, j: (0, j)),
            pl.BlockSpec((d, tf), lambda i, j: (0, j)),
            pl.BlockSpec((tf, d), lambda i, j: (j, 0)),
        ],
        out_specs=pl.BlockSpec((tm, d), lambda i, j: (i, 0)),
        out_shape=jax.ShapeDtypeStruct((t, d), F32),
        scratch_shapes=[pltpu.VMEM((tm, d), BF16)],
        compiler_params=pltpu.CompilerParams(
            dimension_semantics=("parallel", "arbitrary"), vmem_limit_bytes=VMEM_LIMIT),
        name="ffn",
    )(x2d, gain.reshape(1, d), wg, wu, wd)


def _proj_kernel(x_ref, g_ref, w_ref, cos_ref, sin_ref, hg_ref, p_ref, c_ref, xn_ref):
    j = pl.program_id(2)
    scale = HEAD_DIM ** -0.5

    @pl.when(j == 0)
    def _():
        xn_ref[...] = _rms(x_ref[0], g_ref[...]).astype(BF16)

    acc = _mm(xn_ref[...], w_ref[...])

    def blk(c):
        return acc[:, c * LANES:(c + 1) * LANES]

    def rope(v):
        return v * cos_ref[0] + pltpu.roll(v, HEAD_DIM // 2, 1) * sin_ref[0]

    def put(c, v):
        p_ref[0, c] = v.astype(BF16)

    def finish(b, c):
        v = blk(c)
        if b < BLK_KC:
            put(c, rope(_rms(v, hg_ref[0:1, :])) * (scale * LOG2E))
        elif b < BLK_KS:
            v = rope(v) if b < BLK_VC else v
            put(c, v)
            c_ref[0, b - BLK_KC] = v
        elif BLK_KS <= b < BLK_VS:
            put(c, rope(_rms(v, hg_ref[1:2, :])))
        elif BLK_KW <= b < BLK_VW:
            put(c, rope(_rms(v, hg_ref[2:3, :])))
        elif BLK_QS <= b < BLK_KSB:
            put(c, v * (-scale * LOG2E))
        else:
            put(c, v)

    for step in range(N_BLK // PROJ_NB):
        @pl.when(j == step)
        def _(step=step):
            for c in range(PROJ_NB):
                finish(step * PROJ_NB + c, c)


def _in_proj(x, gain, w, cosf, sinf, head_gains, tm):
    b, s, d = x.shape
    tn = PROJ_NB * LANES
    return pl.pallas_call(
        _proj_kernel,
        grid=(b, s // tm, N_BLK // PROJ_NB),
        in_specs=[
            pl.BlockSpec((1, tm, d), lambda bb, i, j: (bb, i, 0)),
            pl.BlockSpec((1, d), lambda bb, i, j: (0, 0)),
            pl.BlockSpec((d, tn), lambda bb, i, j: (0, j)),
            pl.BlockSpec((1, tm, LANES), lambda bb, i, j: (bb, i, 0)),
            pl.BlockSpec((1, tm, LANES), lambda bb, i, j: (bb, i, 0)),
            pl.BlockSpec((8, LANES), lambda bb, i, j: (0, 0)),
        ],
        out_specs=[
            pl.BlockSpec((1, PROJ_NB, tm, LANES), lambda bb, i, j: (bb, j, i, 0)),
            pl.BlockSpec((1, 4, tm, LANES), lambda bb, i, j: (bb, 0, i, 0)),
        ],
        out_shape=[
            jax.ShapeDtypeStruct((b, N_BLK, s, LANES), BF16),
            jax.ShapeDtypeStruct((b, 4, s, LANES), F32),
        ],
        scratch_shapes=[pltpu.VMEM((tm, d), BF16)],
        compiler_params=pltpu.CompilerParams(
            dimension_semantics=("parallel", "parallel", "arbitrary"), vmem_limit_bytes=VMEM_LIMIT),
        name="in_proj",
    )(x, gain.reshape(1, d), w, cosf, sinf, head_gains)


def _compress_kernel(x_ref, pos_ref, w1_ref, w2_ref, g_ref, o_ref, *, normalize):
    nb = o_ref.shape[2]
    hid = w2_ref.shape[0]
    acc = jnp.zeros((nb, 2 * hid), F32)
    bias = jnp.zeros((8, hid), F32)
    for l in range(CMP_STRIDE):
        rows = x_ref[0, 0, pl.ds(l, nb, stride=CMP_STRIDE), :]
        acc = acc + _mm(rows.astype(BF16), w1_ref[l])
        p_lo = jnp.broadcast_to(pos_ref[l:l + 1, :], (8, HEAD_DIM)).astype(BF16)
        p_hi = jnp.broadcast_to(pos_ref[l + CMP_STRIDE:l + CMP_STRIDE + 1, :], (8, HEAD_DIM)).astype(BF16)
        bias = bias + _mm(p_lo, w1_ref[l, :, :hid]) + _mm(p_hi, w1_ref[l, :, hid:])
    hidden = acc[:, :hid] + pltpu.roll(acc[:, hid:], nb - 1, 0) + bias[0:1, :]
    act = 0.5 * hidden * (1.0 + jnp.tanh(0.7978845608028654 * (hidden + 0.044715 * hidden * hidden * hidden)))
    out = _mm(act.astype(BF16), w2_ref[...])
    if normalize:
        out = _rms(out, g_ref[...])
    row = lax.broadcasted_iota(jnp.int32, out.shape, 0)
    o_ref[0, 0] = jnp.where(row < nb - 1, out, 0.0).astype(o_ref.dtype)


def _compress(src, first_blk, pos, w1cat, w2, gain, normalize, out_dtype):
    b, _, s, _ = src.shape
    nb = s // CMP_STRIDE
    hid = w2.shape[0]
    return pl.pallas_call(
        functools.partial(_compress_kernel, normalize=normalize),
        grid=(b, N_NSA_KV),
        in_specs=[
            pl.BlockSpec((1, 1, s, LANES), lambda bb, g: (bb, first_blk + g, 0, 0)),
            pl.BlockSpec((CMP_BLOCK, HEAD_DIM), lambda bb, g: (0, 0)),
            pl.BlockSpec((CMP_STRIDE, HEAD_DIM, 2 * hid), lambda bb, g: (0, 0, 0)),
            pl.BlockSpec((hid, HEAD_DIM), lambda bb, g: (0, 0)),
            pl.BlockSpec((1, HEAD_DIM), lambda bb, g: (0, 0)),
        ],
        out_specs=pl.BlockSpec((1, 1, nb, HEAD_DIM), lambda bb, g: (bb, g, 0, 0)),
        out_shape=jax.ShapeDtypeStruct((b, N_NSA_KV, nb, HEAD_DIM), out_dtype),
        compiler_params=pltpu.CompilerParams(
            dimension_semantics=("parallel", "parallel"), vmem_limit_bytes=VMEM_LIMIT),
        name="compress_k" if normalize else "compress_v",
    )(src, pos, w1cat, w2, gain.reshape(1, HEAD_DIM))


def _nsa_kernel(q_ref, kc_ref, vc_ref, ks_ref, vs_ref, kw_ref, vw_ref, gate_ref, gain_ref,
                o_ref, kaug_ref, vaug_ref, vwaug_ref, qaug_ref, m_ref, acc_ref, mix_ref, sc_ref, *, tq):
    i = pl.program_id(2)
    r = NSA_GROUP
    q0 = i * tq
    s_len = ks_ref.shape[2]
    ncb = kc_ref.shape[2]
    n_slc = s_len // SLC_BLOCK
    nsr = -(-n_slc // 8) * 8
    heads = range(r)

    @pl.when(i == 0)
    def _():
        lane = lax.broadcasted_iota(jnp.int32, (s_len, LANES), 1)
        kblk = lax.broadcasted_iota(jnp.int32, (s_len, LANES), 0) // SLC_BLOCK
        kaug_ref[:, :HEAD_DIM] = ks_ref[0, 0]
        kaug_ref[:, HEAD_DIM:] = jnp.where(lane == kblk, MASK_SCORE, 0.0).astype(BF16)
        ones = jnp.ones((s_len, LANES), BF16)
        vaug_ref[:, :HEAD_DIM] = vs_ref[0, 0]
        vaug_ref[:, HEAD_DIM:] = ones
        vwaug_ref[:, :HEAD_DIM] = vw_ref[0, 0]
        vwaug_ref[:, HEAD_DIM:] = ones

    row = lax.broadcasted_iota(jnp.int32, (tq, tq), 0)
    col = lax.broadcasted_iota(jnp.int32, (tq, tq), 1)
    causal = col <= row
    gt = 1.0 / (1.0 + jnp.exp(-gate_ref[0, 0].astype(F32)))

    def gate(h, c):
        return gt[:, 3 * h + c:3 * h + c + 1]

    kc = kc_ref[0, 0].astype(BF16)
    vc = vc_ref[0, 0]
    cend = lax.broadcasted_iota(jnp.int32, (tq, ncb), 1) * CMP_STRIDE + (CMP_BLOCK - 1)
    cvalid = cend <= q0 + lax.broadcasted_iota(jnp.int32, (tq, ncb), 0)
    imp = jnp.zeros((tq, ncb), F32)
    for h in heads:
        s = jnp.where(cvalid, _nt(q_ref[0, h], kc), NEG_INF)
        e = jnp.where(cvalid, jnp.exp2(s - jnp.max(s, axis=-1, keepdims=True)), 0.0)
        den = jnp.sum(e, axis=-1, keepdims=True)
        pc = e * (1.0 / jnp.where(den > 0.0, den, 1.0))
        imp = imp + pc
        mix_ref[h * tq:(h + 1) * tq, :] = gate(h, 0) * _mm(pc.astype(BF16), vc)

    nsb = nsr
    imp_hi = imp.astype(BF16)
    imp_lo = (imp - imp_hi.astype(F32)).astype(BF16)
    brow = lax.broadcasted_iota(jnp.int32, (nsb, ncb), 0)
    ccol = lax.broadcasted_iota(jnp.int32, (nsb, ncb), 1)
    ratio = SLC_BLOCK // CMP_STRIDE
    span = CMP_BLOCK // CMP_STRIDE
    cover = ((ccol >= ratio * brow - (span - 1)) & (ccol <= ratio * brow + ratio - 1)
             & (ccol < ncb - 1)).astype(BF16)
    p_slc = _nt(cover, imp_hi) + _nt(cover, imp_lo)
    bidx = lax.broadcasted_iota(jnp.int32, (nsb, tq), 0)
    t_lane = q0 + lax.broadcasted_iota(jnp.int32, (nsb, tq), 1)
    cur = t_lane // SLC_BLOCK
    forced = (bidx == 0) | (bidx == cur) | (bidx == cur - 1)
    score = jnp.where(bidx <= cur, jnp.where(forced, FORCE_SCORE, p_slc), NEG_INF)
    score = jnp.where(bidx < n_slc, score, -3e38)
    sc_ref[...] = score
    ranks = []
    for c0 in range(0, nsb, 8):
        mine = score[c0:c0 + 8, :]
        brow8 = c0 + lax.broadcasted_iota(jnp.int32, (8, tq), 0)
        rank = jnp.zeros((8, tq), F32)
        for bp in range(n_slc):
            other = sc_ref[bp:bp + 1, :]
            ge = jnp.where(other >= mine, 1.0, 0.0)
            gt_ = jnp.where(other > mine, 1.0, 0.0)
            if bp < c0:
                rank = rank + ge
            elif bp >= c0 + 8:
                rank = rank + gt_
            else:
                rank = rank + jnp.where(brow8 > bp, ge, gt_)
        ranks.append(rank)
    rank = jnp.concatenate(ranks, axis=0)
    sel_t = jnp.where((rank < float(min(N_SELECT, n_slc))) & (bidx <= cur) & (bidx < n_slc), 0.0, -1.0)
    if nsb < LANES:
        sel_t = jnp.concatenate([sel_t, jnp.full((LANES - nsb, tq), -1.0, F32)], axis=0)
    selm1 = jnp.transpose(sel_t).astype(BF16)
    for h in heads:
        qaug_ref[h * tq:(h + 1) * tq, :HEAD_DIM] = q_ref[0, h]
        qaug_ref[h * tq:(h + 1) * tq, HEAD_DIM:] = selm1

    def init():
        m_ref[...] = jnp.full(m_ref.shape, NEG_INF, F32)
        acc_ref[...] = jnp.zeros(acc_ref.shape, F32)

    def flash(q_of, k, v, mask):
        width = k.shape[0]
        s = [_nt(q_of(h), k) for h in heads]
        p, alpha = [], []
        for h in heads:
            rows = pl.ds(h * tq, tq)
            sh = s[h] if mask is None else jnp.where(mask, s[h], NEG_INF)
            m_prev = m_ref[rows, :]
            m_next = jnp.maximum(m_prev, jnp.max(sh, axis=-1, keepdims=True))
            p.append(jnp.exp2(sh - jnp.tile(m_next, (1, width // LANES))).astype(BF16))
            alpha.append(jnp.exp2(m_prev - m_next))
            m_ref[rows, :] = m_next
        for h in heads:
            rows = pl.ds(h * tq, tq)
            acc_ref[rows, :] = jnp.tile(alpha[h], (1, 2)) * acc_ref[rows, :] + _mm(p[h], v)

    def result(h):
        rows = pl.ds(h * tq, tq)
        return acc_ref[rows, :HEAD_DIM] / acc_ref[rows, HEAD_DIM:]

    def kv_rows(j):
        return pl.ds(pl.multiple_of(j * tq, tq), tq)

    init()

    def q_slc(h):
        return qaug_ref[h * tq:(h + 1) * tq, :]

    def slc_body(j, carry):
        rows = pl.ds(pl.multiple_of(j * (2 * tq), 2 * tq), 2 * tq)
        flash(q_slc, kaug_ref[rows, :], vaug_ref[rows, :], None)
        return carry

    lax.fori_loop(0, i // 2, slc_body, 0)

    @pl.when(i % 2 == 1)
    def _():
        flash(q_slc, kaug_ref[kv_rows(i - 1), :], vaug_ref[kv_rows(i - 1), :], None)

    flash(q_slc, kaug_ref[kv_rows(i), :], vaug_ref[kv_rows(i), :], causal)
    for h in heads:
        mix_ref[h * tq:(h + 1) * tq, :] += gate(h, 1) * result(h)

    nback = WINDOW // tq

    def window(back):
        width = (back + 1) * tq
        rows = pl.ds(pl.multiple_of((i - back) * tq, tq), width)
        k = kw_ref[0, 0, rows, :]
        v = vwaug_ref[rows, :]
        ahead = (lax.broadcasted_iota(jnp.int32, (tq, width), 1)
                 - lax.broadcasted_iota(jnp.int32, (tq, width), 0)) - back * tq
        mask = (ahead <= 0) & (ahead > -WINDOW)
        s = [jnp.where(mask, _nt(q_ref[0, h], k), NEG_INF) for h in heads]
        p = [jnp.exp2(sh - jnp.max(sh, axis=-1, keepdims=True)).astype(BF16) for sh in s]
        o = [_mm(ph, v) for ph in p]
        for h in heads:
            merged = mix_ref[h * tq:(h + 1) * tq, :] + gate(h, 2) * (o[h][:, :HEAD_DIM] / o[h][:, HEAD_DIM:])
            o_ref[0, :, h * HEAD_DIM:(h + 1) * HEAD_DIM] = _rms(merged, gain_ref[0, h:h + 1, :]).astype(BF16)

    for back in range(nback):
        @pl.when(i == back)
        def _(back=back):
            window(back)

    @pl.when(i >= nback)
    def _():
        window(nback)


def _nsa(p, kc, vc, gains, tq):
    b, _, s, _ = p.shape
    g, r = N_NSA_KV, NSA_GROUP
    ncb = kc.shape[2]
    nsr = -(-(s // SLC_BLOCK) // 8) * 8

    def kv_spec(first):
        return pl.BlockSpec((1, 1, s, LANES), lambda bb, gg, i: (bb, first + gg, 0, 0))

    return pl.pallas_call(
        functools.partial(_nsa_kernel, tq=tq),
        grid=(b, g, s // tq),
        in_specs=[
            pl.BlockSpec((1, r, tq, LANES), lambda bb, gg, i: (bb, gg, i, 0)),
            pl.BlockSpec((1, 1, ncb, LANES), lambda bb, gg, i: (bb, gg, 0, 0)),
            pl.BlockSpec((1, 1, ncb, LANES), lambda bb, gg, i: (bb, gg, 0, 0)),
            kv_spec(BLK_KS), kv_spec(BLK_VS), kv_spec(BLK_KW), kv_spec(BLK_VW),
            pl.BlockSpec((1, 1, tq, LANES), lambda bb, gg, i: (bb, BLK_GATE + gg, i, 0)),
            pl.BlockSpec((1, r, HEAD_DIM), lambda bb, gg, i: (gg, 0, 0)),
        ],
        out_specs=pl.BlockSpec((1, tq, r * HEAD_DIM), lambda bb, gg, i: (bb, i, gg)),
        out_shape=jax.ShapeDtypeStruct((b, s, N_NSA_HEADS * HEAD_DIM), BF16),
        scratch_shapes=[
            pltpu.VMEM((s, 2 * LANES), BF16),
            pltpu.VMEM((s, 2 * LANES), BF16),
            pltpu.VMEM((s, 2 * LANES), BF16),
            pltpu.VMEM((r * tq, 2 * LANES), BF16),
            pltpu.VMEM((r * tq, LANES), F32),
            pltpu.VMEM((r * tq, 2 * LANES), F32),
            pltpu.VMEM((r * tq, HEAD_DIM), F32),
            pltpu.VMEM((nsr, tq), F32),
        ],
        compiler_params=pltpu.CompilerParams(
            dimension_semantics=("parallel", "parallel", "arbitrary"), vmem_limit_bytes=VMEM_LIMIT),
        name="nsa",
    )(p, kc, vc, p, p, p, p, p, gains)


def _sb_kernel(q_ref, k_ref, v_ref, tri_ref, gain_ref, o_ref, acc_ref, c_ref, *, tk, parts):
    i = pl.program_id(2)
    acc_ref[...] = jnp.zeros(acc_ref.shape, F32)
    c_ref[...] = jnp.zeros(c_ref.shape, F32)
    tri = tri_ref[...]
    row = lax.broadcasted_iota(jnp.int32, (tk, tk), 0)
    col = lax.broadcasted_iota(jnp.int32, (tk, tk), 1)
    strict = col < row

    def tiles(k, v, work):
        rows = [pl.ds(part * tk, tk) for part, _ in work]
        zn = [_nt(q_ref[0, 0, r, :], k) for r in rows]
        lk = []
        for z, (_, masked) in zip(zn, work):
            x = jnp.minimum(z, 0.0) - jnp.log2(1.0 + jnp.exp2(-jnp.abs(z)))
            lk.append(jnp.where(strict, x, 0.0) if masked else x)
        hi = [x.astype(BF16) for x in lk]
        lo = [(x - h.astype(F32)).astype(BF16) for x, h in zip(lk, hi)]
        incl = [_mm(jnp.concatenate([h, l], axis=1), tri) for h, l in zip(hi, lo)]
        a = []
        for n, z, r, (_, masked) in zip(incl, zn, rows, work):
            x = jnp.exp2(n + jnp.tile(c_ref[r, :], (1, tk // LANES)) - z)
            a.append((jnp.where(strict, x, 0.0) if masked else x).astype(BF16))
        for x, r in zip(a, rows):
            acc_ref[r, :] += _mm(x, v)
        for x, r in zip(lk, rows):
            c_ref[r, :] += jnp.sum(x, axis=-1, keepdims=True)

    def kv(j):
        k0 = pl.multiple_of(j * tk, tk)
        return k_ref[0, 0, pl.ds(k0, tk), :], v_ref[0, 0, pl.ds(k0, tk), :]

    for d in reversed(range(parts)):
        k, v = kv(parts * i + d)
        tiles(k, v, [(d, True)] + [(p, False) for p in range(d + 1, parts)])

    def body(jj, carry):
        k, v = kv(parts * i - 1 - jj)
        tiles(k, v, [(p, False) for p in range(parts)])
        return carry

    lax.fori_loop(0, parts * i, body, 0)
    o_ref[0] = _rms(acc_ref[...], gain_ref[0]).astype(BF16)


def _sb(p, tri, gains, tk, parts):
    b, _, s, _ = p.shape
    tq = parts * tk
    return pl.pallas_call(
        functools.partial(_sb_kernel, tk=tk, parts=parts),
        grid=(b, N_SB_HEADS, s // tq),
        in_specs=[
            pl.BlockSpec((1, 1, tq, LANES), lambda bb, h, i: (bb, BLK_QS + h, i, 0)),
            pl.BlockSpec((1, 1, s, LANES), lambda bb, h, i: (bb, BLK_KSB + h, 0, 0)),
            pl.BlockSpec((1, 1, s, LANES), lambda bb, h, i: (bb, BLK_VSB + h, 0, 0)),
            pl.BlockSpec((2 * tk, tk), lambda bb, h, i: (0, 0)),
            pl.BlockSpec((1, 1, HEAD_DIM), lambda bb, h, i: (h, 0, 0)),
        ],
        out_specs=pl.BlockSpec((1, tq, HEAD_DIM), lambda bb, h, i: (bb, i, h)),
        out_shape=jax.ShapeDtypeStruct((b, s, N_SB_HEADS * HEAD_DIM), BF16),
        scratch_shapes=[
            pltpu.VMEM((tq, HEAD_DIM), F32),
            pltpu.VMEM((tq, LANES), F32),
        ],
        compiler_params=pltpu.CompilerParams(
            dimension_semantics=("parallel", "parallel", "arbitrary"), vmem_limit_bytes=VMEM_LIMIT),
        name="sb",
    )(p, p, p, tri, gains)


def _out_kernel(x_ref, ya_ref, yb_ref, wa_ref, wb_ref, o_ref):
    o_ref[...] = x_ref[...] + _mm(ya_ref[...], wa_ref[...]) + _mm(yb_ref[...], wb_ref[...])


def _out_proj(x2d, ya, yb, wa, wb, tm):
    t, d = x2d.shape
    ka, kb = ya.shape[1], yb.shape[1]
    return pl.pallas_call(
        _out_kernel,
        grid=(t // tm,),
        in_specs=[
            pl.BlockSpec((tm, d), lambda i: (i, 0)),
            pl.BlockSpec((tm, ka), lambda i: (i, 0)),
            pl.BlockSpec((tm, kb), lambda i: (i, 0)),
            pl.BlockSpec((ka, d), lambda i: (0, 0)),
            pl.BlockSpec((kb, d), lambda i: (0, 0)),
        ],
        out_specs=pl.BlockSpec((tm, d), lambda i: (i, 0)),
        out_shape=jax.ShapeDtypeStruct((t, d), F32),
        compiler_params=pltpu.CompilerParams(
            dimension_semantics=("parallel",), vmem_limit_bytes=VMEM_LIMIT),
        name="out_proj",
    )(x2d, ya, yb, wa, wb)


def _tile(n, pref):
    t = min(n, pref)
    while n % t:
        t //= 2
    return t


def _pack_w_in(w_in):
    d = w_in.shape[0]
    hd = HEAD_DIM
    kv = N_NSA_KV * hd
    nq = N_NSA_HEADS * hd
    o_gate = nq + 6 * kv
    o_sb = o_gate + N_NSA_HEADS * 3
    gate = w_in[:, o_gate:o_sb].reshape(d, N_NSA_KV, NSA_GROUP * 3)
    gate = jnp.pad(gate, ((0, 0), (0, 0), (0, hd - NSA_GROUP * 3))).reshape(d, N_NSA_KV * hd)
    pad = jnp.zeros((d, (BLK_QS - BLK_GATE - N_NSA_KV) * hd), w_in.dtype)
    return jnp.concatenate([w_in[:, :o_gate], gate, pad, w_in[:, o_sb:]], axis=1).astype(BF16)


def kernel(x, positions, ffn1_norm, ffn1_w_gate, ffn1_w_up, ffn1_w_down, mix_norm, w_in,
           nsa_q_norm, nsa_k_cmp_norm, nsa_k_slc_norm, nsa_k_win_norm,
           cmp_k_pos, cmp_k_w1, cmp_k_w2, cmp_v_pos, cmp_v_w1, cmp_v_w2,
           nsa_out_norm, sb_out_norm, w_out, ffn2_norm, ffn2_w_gate, ffn2_w_up, ffn2_w_down):
    b, s, d = x.shape
    depth = w_in.shape[0]
    hd = HEAD_DIM
    hid = cmp_k_w2.shape[1]
    t = b * s
    tm = _tile(t, 512)
    tf = _tile(ffn1_w_gate.shape[2], 512)
    tq_nsa = 256
    tk_sb, parts_sb = 256, 8
    assert s % (tk_sb * parts_sb) == 0 and s // SLC_BLOCK <= LANES and WINDOW % tq_nsa == 0 and s % tq_nsa == 0

    inv_freq = ROPE_THETA ** (-jnp.arange(0, hd, 2, dtype=F32) / hd)
    ang = positions.astype(F32)[..., None] * inv_freq
    cosf = jnp.concatenate([jnp.cos(ang), jnp.cos(ang)], axis=-1)
    sinf = jnp.concatenate([-jnp.sin(ang), jnp.sin(ang)], axis=-1)

    idx = jnp.arange(tk_sb, dtype=jnp.int32)
    tri = (idx[:, None] >= idx[None, :]).astype(BF16)
    tri = jnp.concatenate([tri, tri], axis=0)

    def w1cat(w1):
        w = w1.reshape(CMP_BLOCK, hd, hid)
        return jnp.concatenate([w[:CMP_STRIDE], w[CMP_STRIDE:]], axis=-1).astype(BF16)

    for l in range(depth):
        x2 = _ffn(x.reshape(t, d), ffn1_norm[l], ffn1_w_gate[l].astype(BF16), ffn1_w_up[l].astype(BF16),
                  ffn1_w_down[l].astype(BF16), tm, tf)

        head_gains = jnp.zeros((8, hd), F32)
        head_gains = head_gains.at[0].set(nsa_q_norm[l]).at[1].set(nsa_k_slc_norm[l]).at[2].set(nsa_k_win_norm[l])
        p, csrc = _in_proj(x2.reshape(b, s, d), mix_norm[l], _pack_w_in(w_in[l]), cosf, sinf, head_gains, tm=_tile(s, 1024))

        kc = _compress(csrc, 0, cmp_k_pos[l], w1cat(cmp_k_w1[l]), cmp_k_w2[l].astype(BF16),
                       nsa_k_cmp_norm[l], True, F32)
        vc = _compress(csrc, N_NSA_KV, cmp_v_pos[l], w1cat(cmp_v_w1[l]), cmp_v_w2[l].astype(BF16),
                       jnp.ones((hd,), F32), False, BF16)

        y_nsa = _nsa(p, kc, vc, nsa_out_norm[l].reshape(N_NSA_KV, NSA_GROUP, hd), tq_nsa)
        y_sb = _sb(p, tri, sb_out_norm[l].reshape(N_SB_HEADS, 1, hd), tk_sb, parts_sb)

        nmix = N_NSA_HEADS * hd
        x3 = _out_proj(x2, y_nsa.reshape(t, nmix), y_sb.reshape(t, N_SB_HEADS * hd),
                       w_out[l, :nmix].astype(BF16), w_out[l, nmix:].astype(BF16), tm)

        x = _ffn(x3, ffn2_norm[l], ffn2_w_gate[l].astype(BF16), ffn2_w_up[l].astype(BF16),
                 ffn2_w_down[l].astype(BF16), tm, tf).reshape(b, s, d)
    return x
```

```python
import functools

import jax
import jax.numpy as jnp
from jax import lax
from jax.experimental import pallas as pl
from jax.experimental.pallas import tpu as pltpu

HEAD_DIM = 128
N_NSA_HEADS = 8
N_NSA_KV = 2
NSA_GROUP = N_NSA_HEADS // N_NSA_KV
N_SB_HEADS = 8
CMP_BLOCK = 32
CMP_STRIDE = 16
SLC_BLOCK = 64
N_SELECT = 16
WINDOW = 512
ROPE_THETA = 10000.0
EPS = 1e-6
NEG_INF = -1e30
FORCE_SCORE = 1e9
LOG2E = 1.4426950408889634
MASK_SCORE = 1e30

LANES = 128
VMEM_LIMIT = 56 * 1024 * 1024

BLK_Q = 0
BLK_KC, BLK_VC = 8, 10
BLK_KS, BLK_VS = 12, 14
BLK_KW, BLK_VW = 16, 18
BLK_GATE = 20
BLK_QS, BLK_KSB, BLK_VSB = 24, 32, 40
N_BLK = 48
PROJ_NB = 8

BF16 = jnp.bfloat16
F32 = jnp.float32


def _nt(a, b):
    return lax.dot_general(a, b, (((1,), (1,)), ((), ())), preferred_element_type=F32)


def _mm(a, b):
    return jnp.dot(a, b, preferred_element_type=F32)


def _rms(v, g):
    return v * lax.rsqrt(jnp.mean(v * v, axis=-1, keepdims=True) + EPS) * g


def _ffn_kernel(x_ref, g_ref, wg_ref, wu_ref, wd_ref, o_ref, xn_ref):
    j = pl.program_id(1)

    @pl.when(j == 0)
    def _():
        xn_ref[...] = _rms(x_ref[...], g_ref[...]).astype(BF16)
        o_ref[...] = jnp.zeros_like(o_ref)

    xn = xn_ref[...]
    h = _mm(xn, wg_ref[...])
    u = _mm(xn, wu_ref[...])
    a = (h * (1.0 / (1.0 + jnp.exp(-h)))) * u
    o_ref[...] += _mm(a.astype(BF16), wd_ref[...])

    @pl.when(j == pl.num_programs(1) - 1)
    def _():
        o_ref[...] = x_ref[...] + 0.5 * o_ref[...]


def _ffn(x2d, gain, wg, wu, wd, tm, tf):
    t, d = x2d.shape
    ff = wg.shape[1]
    return pl.pallas_call(
        _ffn_kernel,
        grid=(t // tm, ff // tf),
        in_specs=[
            pl.BlockSpec((tm, d), lambda i, j: (i, 0)),
            pl.BlockSpec((1, d), lambda i, j: (0, 0)),
            pl.BlockSpec((d, tf), lambda i, j: (0, j)),
            pl.BlockSpec((d, tf), lambda i, j: (0, j)),
            pl.BlockSpec((tf, d), lambda i, j: (j, 0)),
        ],
        out_specs=pl.BlockSpec((tm, d), lambda i, j: (i, 0)),
        out_shape=jax.ShapeDtypeStruct((t, d), F32),
        scratch_shapes=[pltpu.VMEM((tm, d), BF16)],
        compiler_params=pltpu.CompilerParams(
            dimension_semantics=("parallel", "arbitrary"), vmem_limit_bytes=VMEM_LIMIT),
        name="ffn",
    )(x2d, gain.reshape(1, d), wg, wu, wd)


def _proj_kernel(x_ref, g_ref, w_ref, cos_ref, sin_ref, hg_ref, p_ref, c_ref, xn_ref):
    j = pl.program_id(2)
    scale = HEAD_DIM ** -0.5

    @pl.when(j == 0)
    def _():
        xn_ref[...] = _rms(x_ref[0], g_ref[...]).astype(BF16)

    acc = _mm(xn_ref[...], w_ref[...])

    def blk(c):
        return acc[:, c * LANES:(c + 1) * LANES]

    def rope(v):
        return v * cos_ref[0] + pltpu.roll(v, HEAD_DIM // 2, 1) * sin_ref[0]

    def put(c, v):
        p_ref[0, c] = v.astype(BF16)

    def finish(b, c):
        v = blk(c)
        if b < BLK_KC:
            put(c, rope(_rms(v, hg_ref[0:1, :])))
        elif b < BLK_KS:
            v = rope(v) if b < BLK_VC else v
            put(c, v)
            c_ref[0, b - BLK_KC] = v
        elif BLK_KS <= b < BLK_VS:
            put(c, rope(_rms(v, hg_ref[1:2, :])))
        elif BLK_KW <= b < BLK_VW:
            put(c, rope(_rms(v, hg_ref[2:3, :])))
        elif BLK_QS <= b < BLK_KSB:
            put(c, v * (-scale * LOG2E))
        else:
            put(c, v)

    for step in range(N_BLK // PROJ_NB):
        @pl.when(j == step)
        def _(step=step):
            for c in range(PROJ_NB):
                finish(step * PROJ_NB + c, c)


def _in_proj(x, gain, w, cosf, sinf, head_gains, tm):
    b, s, d = x.shape
    tn = PROJ_NB * LANES
    return pl.pallas_call(
        _proj_kernel,
        grid=(b, s // tm, N_BLK // PROJ_NB),
        in_specs=[
            pl.BlockSpec((1, tm, d), lambda bb, i, j: (bb, i, 0)),
            pl.BlockSpec((1, d), lambda bb, i, j: (0, 0)),
            pl.BlockSpec((d, tn), lambda bb, i, j: (0, j)),
            pl.BlockSpec((1, tm, LANES), lambda bb, i, j: (bb, i, 0)),
            pl.BlockSpec((1, tm, LANES), lambda bb, i, j: (bb, i, 0)),
            pl.BlockSpec((8, LANES), lambda bb, i, j: (0, 0)),
        ],
        out_specs=[
            pl.BlockSpec((1, PROJ_NB, tm, LANES), lambda bb, i, j: (bb, j, i, 0)),
            pl.BlockSpec((1, 4, tm, LANES), lambda bb, i, j: (bb, 0, i, 0)),
        ],
        out_shape=[
            jax.ShapeDtypeStruct((b, N_BLK, s, LANES), BF16),
            jax.ShapeDtypeStruct((b, 4, s, LANES), F32),
        ],
        scratch_shapes=[pltpu.VMEM((tm, d), BF16)],
        compiler_params=pltpu.CompilerParams(
            dimension_semantics=("parallel", "parallel", "arbitrary"), vmem_limit_bytes=VMEM_LIMIT),
        name="in_proj",
    )(x, gain.reshape(1, d), w, cosf, sinf, head_gains)


def _compress_kernel(x_ref, pos_ref, w1_ref, w2_ref, g_ref, o_ref, *, normalize):
    nb = o_ref.shape[2]
    hid = w2_ref.shape[0]
    acc = jnp.zeros((nb, 2 * hid), F32)
    bias = jnp.zeros((8, hid), F32)
    for l in range(CMP_STRIDE):
        rows = x_ref[0, 0, pl.ds(l, nb, stride=CMP_STRIDE), :]
        acc = acc + _mm(rows.astype(BF16), w1_ref[l])
        p_lo = jnp.broadcast_to(pos_ref[l:l + 1, :], (8, HEAD_DIM)).astype(BF16)
        p_hi = jnp.broadcast_to(pos_ref[l + CMP_STRIDE:l + CMP_STRIDE + 1, :], (8, HEAD_DIM)).astype(BF16)
        bias = bias + _mm(p_lo, w1_ref[l, :, :hid]) + _mm(p_hi, w1_ref[l, :, hid:])
    hidden = acc[:, :hid] + pltpu.roll(acc[:, hid:], nb - 1, 0) + bias[0:1, :]
    act = 0.5 * hidden * (1.0 + jnp.tanh(0.7978845608028654 * (hidden + 0.044715 * hidden * hidden * hidden)))
    out = _mm(act.astype(BF16), w2_ref[...])
    if normalize:
        out = _rms(out, g_ref[...])
    row = lax.broadcasted_iota(jnp.int32, out.shape, 0)
    o_ref[0, 0] = jnp.where(row < nb - 1, out, 0.0).astype(o_ref.dtype)


def _compress(src, first_blk, pos, w1cat, w2, gain, normalize, out_dtype):
    b, _, s, _ = src.shape
    nb = s // CMP_STRIDE
    hid = w2.shape[0]
    return pl.pallas_call(
        functools.partial(_compress_kernel, normalize=normalize),
        grid=(b, N_NSA_KV),
        in_specs=[
            pl.BlockSpec((1, 1, s, LANES), lambda bb, g: (bb, first_blk + g, 0, 0)),
            pl.BlockSpec((CMP_BLOCK, HEAD_DIM), lambda bb, g: (0, 0)),
            pl.BlockSpec((CMP_STRIDE, HEAD_DIM, 2 * hid), lambda bb, g: (0, 0, 0)),
            pl.BlockSpec((hid, HEAD_DIM), lambda bb, g: (0, 0)),
            pl.BlockSpec((1, HEAD_DIM), lambda bb, g: (0, 0)),
        ],
        out_specs=pl.BlockSpec((1, 1, nb, HEAD_DIM), lambda bb, g: (bb, g, 0, 0)),
        out_shape=jax.ShapeDtypeStruct((b, N_NSA_KV, nb, HEAD_DIM), out_dtype),
        compiler_params=pltpu.CompilerParams(
            dimension_semantics=("parallel", "parallel"), vmem_limit_bytes=VMEM_LIMIT),
        name="compress_k" if normalize else "compress_v",
    )(src, pos, w1cat, w2, gain.reshape(1, HEAD_DIM))


def _nsa_kernel(q_ref, kc_ref, vc_ref, ks_ref, vs_ref, kw_ref, vw_ref, gate_ref, gain_ref,
                o_ref, kaug_ref, vaug_ref, vwaug_ref, qaug_ref, m_ref, acc_ref, mix_ref, sc_ref, *, tq):
    i = pl.program_id(2)
    r = NSA_GROUP
    q0 = i * tq
    s_len = ks_ref.shape[2]
    ncb = kc_ref.shape[2]
    n_slc = s_len // SLC_BLOCK
    nsr = -(-n_slc // 8) * 8
    heads = range(r)

    @pl.when(i == 0)
    def _():
        lane = lax.broadcasted_iota(jnp.int32, (s_len, LANES), 1)
        kblk = lax.broadcasted_iota(jnp.int32, (s_len, LANES), 0) // SLC_BLOCK
        kaug_ref[:, :HEAD_DIM] = ks_ref[0, 0]
        kaug_ref[:, HEAD_DIM:] = jnp.where(lane == kblk, MASK_SCORE, 0.0).astype(BF16)
        ones = jnp.ones((s_len, LANES), BF16)
        vaug_ref[:, :HEAD_DIM] = vs_ref[0, 0]
        vaug_ref[:, HEAD_DIM:] = ones
        vwaug_ref[:, :HEAD_DIM] = vw_ref[0, 0]
        vwaug_ref[:, HEAD_DIM:] = ones

    row = lax.broadcasted_iota(jnp.int32, (tq, tq), 0)
    col = lax.broadcasted_iota(jnp.int32, (tq, tq), 1)
    causal = col <= row
    gt = 1.0 / (1.0 + jnp.exp(-gate_ref[0, 0].astype(F32)))

    def gate(h, c):
        return gt[:, 3 * h + c:3 * h + c + 1]

    kc = kc_ref[0, 0].astype(BF16)
    vc = vc_ref[0, 0]
    cend = lax.broadcasted_iota(jnp.int32, (tq, ncb), 1) * CMP_STRIDE + (CMP_BLOCK - 1)
    cvalid = cend <= q0 + lax.broadcasted_iota(jnp.int32, (tq, ncb), 0)
    imp = jnp.zeros((tq, ncb), F32)
    for h in heads:
        s = jnp.where(cvalid, _nt(q_ref[0, h], kc), NEG_INF)
        e = jnp.where(cvalid, jnp.exp2(s - jnp.max(s, axis=-1, keepdims=True)), 0.0)
        den = jnp.sum(e, axis=-1, keepdims=True)
        pc = e * (1.0 / jnp.where(den > 0.0, den, 1.0))
        imp = imp + pc
        mix_ref[h * tq:(h + 1) * tq, :] = gate(h, 0) * _mm(pc.astype(BF16), vc)

    nsb = nsr
    imp_hi = imp.astype(BF16)
    imp_lo = (imp - imp_hi.astype(F32)).astype(BF16)
    brow = lax.broadcasted_iota(jnp.int32, (nsb, ncb), 0)
    ccol = lax.broadcasted_iota(jnp.int32, (nsb, ncb), 1)
    ratio = SLC_BLOCK // CMP_STRIDE
    span = CMP_BLOCK // CMP_STRIDE
    cover = ((ccol >= ratio * brow - (span - 1)) & (ccol <= ratio * brow + ratio - 1)
             & (ccol < ncb - 1)).astype(BF16)
    p_slc = _nt(cover, imp_hi) + _nt(cover, imp_lo)
    bidx = lax.broadcasted_iota(jnp.int32, (nsb, tq), 0)
    t_lane = q0 + lax.broadcasted_iota(jnp.int32, (nsb, tq), 1)
    cur = t_lane // SLC_BLOCK
    forced = (bidx == 0) | (bidx == cur) | (bidx == cur - 1)
    score = jnp.where(bidx <= cur, jnp.where(forced, FORCE_SCORE, p_slc), NEG_INF)
    score = jnp.where(bidx < n_slc, score, -3e38)
    sc_ref[...] = score
    ranks = []
    for c0 in range(0, nsb, 8):
        mine = score[c0:c0 + 8, :]
        brow8 = c0 + lax.broadcasted_iota(jnp.int32, (8, tq), 0)
        rank = jnp.zeros((8, tq), F32)
        for bp in range(n_slc):
            other = sc_ref[bp:bp + 1, :]
            ge = jnp.where(other >= mine, 1.0, 0.0)
            gt_ = jnp.where(other > mine, 1.0, 0.0)
            if bp < c0:
                rank = rank + ge
            elif bp >= c0 + 8:
                rank = rank + gt_
            else:
                rank = rank + jnp.where(brow8 > bp, ge, gt_)
        ranks.append(rank)
    rank = jnp.concatenate(ranks, axis=0)
    sel_t = jnp.where((rank < float(min(N_SELECT, n_slc))) & (bidx <= cur) & (bidx < n_slc), 0.0, -1.0)
    if nsb < LANES:
        sel_t = jnp.concatenate([sel_t, jnp.full((LANES - nsb, tq), -1.0, F32)], axis=0)
    selm1 = jnp.transpose(sel_t).astype(BF16)
    for h in heads:
        qaug_ref[h * tq:(h + 1) * tq, :HEAD_DIM] = q_ref[0, h]
        qaug_ref[h * tq:(h + 1) * tq, HEAD_DIM:] = selm1

    def init():
        m_ref[...] = jnp.full(m_ref.shape, NEG_INF, F32)
        acc_ref[...] = jnp.zeros(acc_ref.shape, F32)

    def flash(q_of, k, v, mask):
        width = k.shape[0]
        s = [_nt(q_of(h), k) for h in heads]
        p, alpha = [], []
        for h in heads:
            rows = pl.ds(h * tq, tq)
            sh = s[h] if mask is None else jnp.where(mask, s[h], NEG_INF)
            m_prev = m_ref[rows, :]
            m_next = jnp.maximum(m_prev, jnp.max(sh, axis=-1, keepdims=True))
            p.append(jnp.exp2(sh - jnp.tile(m_next, (1, width // LANES))).astype(BF16))
            alpha.append(jnp.exp2(m_prev - m_next))
            m_ref[rows, :] = m_next
        for h in heads:
            rows = pl.ds(h * tq, tq)
            acc_ref[rows, :] = jnp.tile(alpha[h], (1, 2)) * acc_ref[rows, :] + _mm(p[h], v)

    def result(h):
        rows = pl.ds(h * tq, tq)
        return acc_ref[rows, :HEAD_DIM] / acc_ref[rows, HEAD_DIM:]

    def kv_rows(j):
        return pl.ds(pl.multiple_of(j * tq, tq), tq)

    init()

    def q_slc(h):
        return qaug_ref[h * tq:(h + 1) * tq, :]

    def slc_body(j, carry):
        rows = pl.ds(pl.multiple_of(j * (2 * tq), 2 * tq), 2 * tq)
        flash(q_slc, kaug_ref[rows, :], vaug_ref[rows, :], None)
        return carry

    lax.fori_loop(0, i // 2, slc_body, 0)

    @pl.when(i % 2 == 1)
    def _():
        rows = pl.ds(pl.multiple_of((i - 1) * tq, tq), 2 * tq)
        ahead = (lax.broadcasted_iota(jnp.int32, (tq, 2 * tq), 1)
                 - lax.broadcasted_iota(jnp.int32, (tq, 2 * tq), 0)) - tq
        flash(q_slc, kaug_ref[rows, :], vaug_ref[rows, :], ahead <= 0)

    @pl.when(i % 2 == 0)
    def _():
        flash(q_slc, kaug_ref[kv_rows(i), :], vaug_ref[kv_rows(i), :], causal)
    for h in heads:
        mix_ref[h * tq:(h + 1) * tq, :] += gate(h, 1) * result(h)

    nback = WINDOW // tq

    def window(back):
        width = (back + 1) * tq
        rows = pl.ds(pl.multiple_of((i - back) * tq, tq), width)
        k = kw_ref[0, 0, rows, :]
        v = vwaug_ref[rows, :]
        ahead = (lax.broadcasted_iota(jnp.int32, (tq, width), 1)
                 - lax.broadcasted_iota(jnp.int32, (tq, width), 0)) - back * tq
        mask = (ahead <= 0) & (ahead > -WINDOW)
        s = [jnp.where(mask, _nt(q_ref[0, h], k), NEG_INF) for h in heads]
        p = [jnp.exp2(sh - jnp.max(sh, axis=-1, keepdims=True)).astype(BF16) for sh in s]
        o = [_mm(ph, v) for ph in p]
        for h in heads:
            merged = mix_ref[h * tq:(h + 1) * tq, :] + gate(h, 2) * (o[h][:, :HEAD_DIM] / o[h][:, HEAD_DIM:])
            o_ref[0, :, h * HEAD_DIM:(h + 1) * HEAD_DIM] = _rms(merged, gain_ref[0, h:h + 1, :]).astype(BF16)

    for back in range(nback):
        @pl.when(i == back)
        def _(back=back):
            window(back)

    @pl.when(i >= nback)
    def _():
        window(nback)


def _nsa(p, kc, vc, gains, tq):
    b, _, s, _ = p.shape
    g, r = N_NSA_KV, NSA_GROUP
    ncb = kc.shape[2]
    nsr = -(-(s // SLC_BLOCK) // 8) * 8

    def kv_spec(first):
        return pl.BlockSpec((1, 1, s, LANES), lambda bb, gg, i: (bb, first + gg, 0, 0))

    return pl.pallas_call(
        functools.partial(_nsa_kernel, tq=tq),
        grid=(b, g, s // tq),
        in_specs=[
            pl.BlockSpec((1, r, tq, LANES), lambda bb, gg, i: (bb, gg, i, 0)),
            pl.BlockSpec((1, 1, ncb, LANES), lambda bb, gg, i: (bb, gg, 0, 0)),
            pl.BlockSpec((1, 1, ncb, LANES), lambda bb, gg, i: (bb, gg, 0, 0)),
            kv_spec(BLK_KS), kv_spec(BLK_VS), kv_spec(BLK_KW), kv_spec(BLK_VW),
            pl.BlockSpec((1, 1, tq, LANES), lambda bb, gg, i: (bb, BLK_GATE + gg, i, 0)),
            pl.BlockSpec((1, r, HEAD_DIM), lambda bb, gg, i: (gg, 0, 0)),
        ],
        out_specs=pl.BlockSpec((1, tq, r * HEAD_DIM), lambda bb, gg, i: (bb, i, gg)),
        out_shape=jax.ShapeDtypeStruct((b, s, N_NSA_HEADS * HEAD_DIM), BF16),
        scratch_shapes=[
            pltpu.VMEM((s, 2 * LANES), BF16),
            pltpu.VMEM((s, 2 * LANES), BF16),
            pltpu.VMEM((s, 2 * LANES), BF16),
            pltpu.VMEM((r * tq, 2 * LANES), BF16),
            pltpu.VMEM((r * tq, LANES), F32),
            pltpu.VMEM((r * tq, 2 * LANES), F32),
            pltpu.VMEM((r * tq, HEAD_DIM), F32),
            pltpu.VMEM((nsr, tq), F32),
        ],
        compiler_params=pltpu.CompilerParams(
            dimension_semantics=("parallel", "parallel", "arbitrary"), vmem_limit_bytes=VMEM_LIMIT),
        name="nsa",
    )(p, kc, vc, p, p, p, p, p, gains)


def _sb_kernel(q_ref, k_ref, v_ref, tri_ref, gain_ref, o_ref, acc_ref, c_ref, *, tk, parts):
    i = pl.program_id(2)
    acc_ref[...] = jnp.zeros(acc_ref.shape, F32)
    c_ref[...] = jnp.zeros(c_ref.shape, F32)
    tri = tri_ref[...]
    row = lax.broadcasted_iota(jnp.int32, (tk, tk), 0)
    col = lax.broadcasted_iota(jnp.int32, (tk, tk), 1)
    strict = col < row

    def tiles(k, v, work):
        rows = [pl.ds(part * tk, tk) for part, _ in work]
        zn = [_nt(q_ref[0, 0, r, :], k) for r in rows]
        lk = []
        for z, (_, masked) in zip(zn, work):
            x = jnp.minimum(z, 0.0) - jnp.log2(1.0 + jnp.exp2(-jnp.abs(z)))
            lk.append(jnp.where(strict, x, 0.0) if masked else x)
        hi = [x.astype(BF16) for x in lk]
        lo = [(x - h.astype(F32)).astype(BF16) for x, h in zip(lk, hi)]
        incl = [_mm(jnp.concatenate([h, l], axis=1), tri) for h, l in zip(hi, lo)]
        a = []
        for n, z, r, (_, masked) in zip(incl, zn, rows, work):
            x = jnp.exp2(n + jnp.tile(c_ref[r, :], (1, tk // LANES)) - z)
            a.append((jnp.where(strict, x, 0.0) if masked else x).astype(BF16))
        for x, r in zip(a, rows):
            acc_ref[r, :] += _mm(x, v)
        for x, r in zip(lk, rows):
            c_ref[r, :] += jnp.sum(x, axis=-1, keepdims=True)

    def kv(j):
        k0 = pl.multiple_of(j * tk, tk)
        return k_ref[0, 0, pl.ds(k0, tk), :], v_ref[0, 0, pl.ds(k0, tk), :]

    for d in reversed(range(parts)):
        k, v = kv(parts * i + d)
        tiles(k, v, [(d, True)] + [(p, False) for p in range(d + 1, parts)])

    def body(jj, carry):
        k, v = kv(parts * i - 1 - jj)
        tiles(k, v, [(p, False) for p in range(parts)])
        return carry

    lax.fori_loop(0, parts * i, body, 0)
    o_ref[0] = _rms(acc_ref[...], gain_ref[0]).astype(BF16)


def _sb(p, tri, gains, tk, parts):
    b, _, s, _ = p.shape
    tq = parts * tk
    return pl.pallas_call(
        functools.partial(_sb_kernel, tk=tk, parts=parts),
        grid=(b, N_SB_HEADS, s // tq),
        in_specs=[
            pl.BlockSpec((1, 1, tq, LANES), lambda bb, h, i: (bb, BLK_QS + h, i, 0)),
            pl.BlockSpec((1, 1, s, LANES), lambda bb, h, i: (bb, BLK_KSB + h, 0, 0)),
            pl.BlockSpec((1, 1, s, LANES), lambda bb, h, i: (bb, BLK_VSB + h, 0, 0)),
            pl.BlockSpec((2 * tk, tk), lambda bb, h, i: (0, 0)),
            pl.BlockSpec((1, 1, HEAD_DIM), lambda bb, h, i: (h, 0, 0)),
        ],
        out_specs=pl.BlockSpec((1, tq, HEAD_DIM), lambda bb, h, i: (bb, i, h)),
        out_shape=jax.ShapeDtypeStruct((b, s, N_SB_HEADS * HEAD_DIM), BF16),
        scratch_shapes=[
            pltpu.VMEM((tq, HEAD_DIM), F32),
            pltpu.VMEM((tq, LANES), F32),
        ],
        compiler_params=pltpu.CompilerParams(
            dimension_semantics=("parallel", "parallel", "arbitrary"), vmem_limit_bytes=VMEM_LIMIT),
        name="sb",
    )(p, p, p, tri, gains)


def _out_kernel(x_ref, ya_ref, yb_ref, wa_ref, wb_ref, o_ref):
    o_ref[...] = x_ref[...] + _mm(ya_ref[...], wa_ref[...]) + _mm(yb_ref[...], wb_ref[...])


def _out_proj(x2d, ya, yb, wa, wb, tm):
    t, d = x2d.shape
    ka, kb = ya.shape[1], yb.shape[1]
    return pl.pallas_call(
        _out_kernel,
        grid=(t // tm,),
        in_specs=[
            pl.BlockSpec((tm, d), lambda i: (i, 0)),
            pl.BlockSpec((tm, ka), lambda i: (i, 0)),
            pl.BlockSpec((tm, kb), lambda i: (i, 0)),
            pl.BlockSpec((ka, d), lambda i: (0, 0)),
            pl.BlockSpec((kb, d), lambda i: (0, 0)),
        ],
        out_specs=pl.BlockSpec((tm, d), lambda i: (i, 0)),
        out_shape=jax.ShapeDtypeStruct((t, d), F32),
        compiler_params=pltpu.CompilerParams(
            dimension_semantics=("parallel",), vmem_limit_bytes=VMEM_LIMIT),
        name="out_proj",
    )(x2d, ya, yb, wa, wb)


def _tile(n, pref):
    t = min(n, pref)
    while n % t:
        t //= 2
    return t


def _pack_w_in(w_in):
    d = w_in.shape[0]
    hd = HEAD_DIM
    kv = N_NSA_KV * hd
    nq = N_NSA_HEADS * hd
    o_gate = nq + 6 * kv
    o_sb = o_gate + N_NSA_HEADS * 3
    gate = w_in[:, o_gate:o_sb].reshape(d, N_NSA_KV, NSA_GROUP * 3)
    gate = jnp.pad(gate, ((0, 0), (0, 0), (0, hd - NSA_GROUP * 3))).reshape(d, N_NSA_KV * hd)
    pad = jnp.zeros((d, (BLK_QS - BLK_GATE - N_NSA_KV) * hd), w_in.dtype)
    return jnp.concatenate([w_in[:, :o_gate], gate, pad, w_in[:, o_sb:]], axis=1).astype(BF16)


def kernel(x, positions, ffn1_norm, ffn1_w_gate, ffn1_w_up, ffn1_w_down, mix_norm, w_in,
           nsa_q_norm, nsa_k_cmp_norm, nsa_k_slc_norm, nsa_k_win_norm,
           cmp_k_pos, cmp_k_w1, cmp_k_w2, cmp_v_pos, cmp_v_w1, cmp_v_w2,
           nsa_out_norm, sb_out_norm, w_out, ffn2_norm, ffn2_w_gate, ffn2_w_up, ffn2_w_down):
    b, s, d = x.shape
    depth = w_in.shape[0]
    hd = HEAD_DIM
    hid = cmp_k_w2.shape[1]
    t = b * s
    tm = _tile(t, 512)
    tm_ffn = _tile(t, 1024)
    tf = _tile(ffn1_w_gate.shape[2], 512)
    tq_nsa = 256
    tk_sb, parts_sb = 256, 8
    assert s % (tk_sb * parts_sb) == 0 and s // SLC_BLOCK <= LANES and WINDOW % tq_nsa == 0 and s % tq_nsa == 0

    inv_freq = ROPE_THETA ** (-jnp.arange(0, hd, 2, dtype=F32) / hd)
    ang = positions.astype(F32)[..., None] * inv_freq
    cosf = jnp.concatenate([jnp.cos(ang), jnp.cos(ang)], axis=-1)
    sinf = jnp.concatenate([-jnp.sin(ang), jnp.sin(ang)], axis=-1)

    idx = jnp.arange(tk_sb, dtype=jnp.int32)
    tri = (idx[:, None] >= idx[None, :]).astype(BF16)
    tri = jnp.concatenate([tri, tri], axis=0)

    def w1cat(w1):
        w = w1.reshape(CMP_BLOCK, hd, hid)
        return jnp.concatenate([w[:CMP_STRIDE], w[CMP_STRIDE:]], axis=-1).astype(BF16)

    for l in range(depth):
        x2 = _ffn(x.reshape(t, d), ffn1_norm[l], ffn1_w_gate[l].astype(BF16), ffn1_w_up[l].astype(BF16),
                  ffn1_w_down[l].astype(BF16), tm_ffn, tf)

        head_gains = jnp.zeros((8, hd), F32)
        head_gains = head_gains.at[0].set(nsa_q_norm[l] * (hd ** -0.5 * LOG2E)).at[1].set(nsa_k_slc_norm[l]).at[2].set(nsa_k_win_norm[l])
        p, csrc = _in_proj(x2.reshape(b, s, d), mix_norm[l], _pack_w_in(w_in[l]), cosf, sinf, head_gains, tm=_tile(s, 1024))

        kc = _compress(csrc, 0, cmp_k_pos[l], w1cat(cmp_k_w1[l]), cmp_k_w2[l].astype(BF16),
                       nsa_k_cmp_norm[l], True, F32)
        vc = _compress(csrc, N_NSA_KV, cmp_v_pos[l], w1cat(cmp_v_w1[l]), cmp_v_w2[l].astype(BF16),
                       jnp.ones((hd,), F32), False, BF16)

        y_nsa = _nsa(p, kc, vc, nsa_out_norm[l].reshape(N_NSA_KV, NSA_GROUP, hd), tq_nsa)
        y_sb = _sb(p, tri, sb_out_norm[l].reshape(N_SB_HEADS, 1, hd), tk_sb, parts_sb)

        nmix = N_NSA_HEADS * hd
        x3 = _out_proj(x2, y_nsa.reshape(t, nmix), y_sb.reshape(t, N_SB_HEADS * hd),
                       w_out[l, :nmix].astype(BF16), w_out[l, nmix:].astype(BF16), tm)

        x = _ffn(x3, ffn2_norm[l], ffn2_w_gate[l].astype(BF16), ffn2_w_up[l].astype(BF16),
                 ffn2_w_down[l].astype(BF16), tm_ffn, tf).reshape(b, s, d)
    return x
```

```python
import functools

import jax
import jax.numpy as jnp
from jax import lax
from jax.experimental import pallas as pl
from jax.experimental.pallas import tpu as pltpu

HEAD_DIM = 128
N_NSA_HEADS = 8
N_NSA_KV = 2
NSA_GROUP = N_NSA_HEADS // N_NSA_KV
N_SB_HEADS = 8
CMP_BLOCK = 32
CMP_STRIDE = 16
SLC_BLOCK = 64
N_SELECT = 16
WINDOW = 512
ROPE_THETA = 10000.0
EPS = 1e-6
NEG_INF = -1e30
FORCE_SCORE = 1e9
LOG2E = 1.4426950408889634
MASK_SCORE = 1e30

LANES = 128
VMEM_LIMIT = 56 * 1024 * 1024

BLK_Q = 0
BLK_KC, BLK_VC = 8, 10
BLK_KS, BLK_VS = 12, 14
BLK_KW, BLK_VW = 16, 18
BLK_GATE = 20
BLK_QS, BLK_KSB, BLK_VSB = 24, 32, 40
N_BLK = 48
PROJ_NB = 8
SLC_STEP = 4

BF16 = jnp.bfloat16
F32 = jnp.float32


def _nt(a, b):
    return lax.dot_general(a, b, (((1,), (1,)), ((), ())), preferred_element_type=F32)


def _mm(a, b):
    return jnp.dot(a, b, preferred_element_type=F32)


def _rms(v, g):
    return v * lax.rsqrt(jnp.mean(v * v, axis=-1, keepdims=True) + EPS) * g


def _ffn_kernel(x_ref, g_ref, wg_ref, wu_ref, wd_ref, o_ref, xn_ref):
    j = pl.program_id(1)

    @pl.when(j == 0)
    def _():
        xn_ref[...] = _rms(x_ref[...], g_ref[...]).astype(BF16)
        o_ref[...] = jnp.zeros_like(o_ref)

    xn = xn_ref[...]
    h = _mm(xn, wg_ref[...])
    u = _mm(xn, wu_ref[...])
    a = (h * (1.0 / (1.0 + jnp.exp(-h)))) * u
    o_ref[...] += _mm(a.astype(BF16), wd_ref[...])

    @pl.when(j == pl.num_programs(1) - 1)
    def _():
        o_ref[...] = x_ref[...] + 0.5 * o_ref[...]


def _ffn(x2d, gain, wg, wu, wd, tm, tf):
    t, d = x2d.shape
    ff = wg.shape[1]
    return pl.pallas_call(
        _ffn_kernel,
        grid=(t // tm, ff // tf),
        in_specs=[
            pl.BlockSpec((tm, d), lambda i, j: (i, 0)),
            pl.BlockSpec((1, d), lambda i, j: (0, 0)),
            pl.BlockSpec((d, tf), lambda i, j: (0, j)),
            pl.BlockSpec((d, tf), lambda i, j: (0, j)),
            pl.BlockSpec((tf, d), lambda i, j: (j, 0)),
        ],
        out_specs=pl.BlockSpec((tm, d), lambda i, j: (i, 0)),
        out_shape=jax.ShapeDtypeStruct((t, d), F32),
        scratch_shapes=[pltpu.VMEM((tm, d), BF16)],
        compiler_params=pltpu.CompilerParams(
            dimension_semantics=("parallel", "arbitrary"), vmem_limit_bytes=VMEM_LIMIT),
        name="ffn",
    )(x2d, gain.reshape(1, d), wg, wu, wd)


def _proj_kernel(x_ref, g_ref, w_ref, cos_ref, sin_ref, hg_ref, p_ref, c_ref, xn_ref):
    j = pl.program_id(2)
    scale = HEAD_DIM ** -0.5

    @pl.when(j == 0)
    def _():
        xn_ref[...] = _rms(x_ref[0], g_ref[...]).astype(BF16)

    acc = _mm(xn_ref[...], w_ref[...])

    def blk(c):
        return acc[:, c * LANES:(c + 1) * LANES]

    def rope(v):
        return v * cos_ref[0] + pltpu.roll(v, HEAD_DIM // 2, 1) * sin_ref[0]

    def put(c, v):
        p_ref[0, c] = v.astype(BF16)

    def finish(b, c):
        v = blk(c)
        if b < BLK_KC:
            put(c, rope(_rms(v, hg_ref[0:1, :])))
        elif b < BLK_KS:
            v = rope(v) if b < BLK_VC else v
            put(c, v)
            c_ref[0, b - BLK_KC] = v
        elif BLK_KS <= b < BLK_VS:
            put(c, rope(_rms(v, hg_ref[1:2, :])))
        elif BLK_KW <= b < BLK_VW:
            put(c, rope(_rms(v, hg_ref[2:3, :])))
        elif BLK_QS <= b < BLK_KSB:
            put(c, v * (-scale * LOG2E))
        else:
            put(c, v)

    for step in range(N_BLK // PROJ_NB):
        @pl.when(j == step)
        def _(step=step):
            for c in range(PROJ_NB):
                finish(step * PROJ_NB + c, c)


def _in_proj(x, gain, w, cosf, sinf, head_gains, tm):
    b, s, d = x.shape
    tn = PROJ_NB * LANES
    return pl.pallas_call(
        _proj_kernel,
        grid=(b, s // tm, N_BLK // PROJ_NB),
        in_specs=[
            pl.BlockSpec((1, tm, d), lambda bb, i, j: (bb, i, 0)),
            pl.BlockSpec((1, d), lambda bb, i, j: (0, 0)),
            pl.BlockSpec((d, tn), lambda bb, i, j: (0, j)),
            pl.BlockSpec((1, tm, LANES), lambda bb, i, j: (bb, i, 0)),
            pl.BlockSpec((1, tm, LANES), lambda bb, i, j: (bb, i, 0)),
            pl.BlockSpec((8, LANES), lambda bb, i, j: (0, 0)),
        ],
        out_specs=[
            pl.BlockSpec((1, PROJ_NB, tm, LANES), lambda bb, i, j: (bb, j, i, 0)),
            pl.BlockSpec((1, 4, tm, LANES), lambda bb, i, j: (bb, 0, i, 0)),
        ],
        out_shape=[
            jax.ShapeDtypeStruct((b, N_BLK, s, LANES), BF16),
            jax.ShapeDtypeStruct((b, 4, s, LANES), F32),
        ],
        scratch_shapes=[pltpu.VMEM((tm, d), BF16)],
        compiler_params=pltpu.CompilerParams(
            dimension_semantics=("parallel", "parallel", "arbitrary"), vmem_limit_bytes=VMEM_LIMIT),
        name="in_proj",
    )(x, gain.reshape(1, d), w, cosf, sinf, head_gains)


def _compress_kernel(x_ref, pos_ref, w1_ref, w2_ref, g_ref, o_ref, *, normalize):
    nb = o_ref.shape[2]
    hid = w2_ref.shape[0]
    acc = jnp.zeros((nb, 2 * hid), F32)
    bias = jnp.zeros((8, hid), F32)
    for l in range(CMP_STRIDE):
        rows = x_ref[0, 0, pl.ds(l, nb, stride=CMP_STRIDE), :]
        acc = acc + _mm(rows.astype(BF16), w1_ref[l])
        p_lo = jnp.broadcast_to(pos_ref[l:l + 1, :], (8, HEAD_DIM)).astype(BF16)
        p_hi = jnp.broadcast_to(pos_ref[l + CMP_STRIDE:l + CMP_STRIDE + 1, :], (8, HEAD_DIM)).astype(BF16)
        bias = bias + _mm(p_lo, w1_ref[l, :, :hid]) + _mm(p_hi, w1_ref[l, :, hid:])
    hidden = acc[:, :hid] + pltpu.roll(acc[:, hid:], nb - 1, 0) + bias[0:1, :]
    act = 0.5 * hidden * (1.0 + jnp.tanh(0.7978845608028654 * (hidden + 0.044715 * hidden * hidden * hidden)))
    out = _mm(act.astype(BF16), w2_ref[...])
    if normalize:
        out = _rms(out, g_ref[...])
    row = lax.broadcasted_iota(jnp.int32, out.shape, 0)
    o_ref[0, 0] = jnp.where(row < nb - 1, out, 0.0).astype(o_ref.dtype)


def _compress(src, first_blk, pos, w1cat, w2, gain, normalize, out_dtype):
    b, _, s, _ = src.shape
    nb = s // CMP_STRIDE
    hid = w2.shape[0]
    return pl.pallas_call(
        functools.partial(_compress_kernel, normalize=normalize),
        grid=(b, N_NSA_KV),
        in_specs=[
            pl.BlockSpec((1, 1, s, LANES), lambda bb, g: (bb, first_blk + g, 0, 0)),
            pl.BlockSpec((CMP_BLOCK, HEAD_DIM), lambda bb, g: (0, 0)),
            pl.BlockSpec((CMP_STRIDE, HEAD_DIM, 2 * hid), lambda bb, g: (0, 0, 0)),
            pl.BlockSpec((hid, HEAD_DIM), lambda bb, g: (0, 0)),
            pl.BlockSpec((1, HEAD_DIM), lambda bb, g: (0, 0)),
        ],
        out_specs=pl.BlockSpec((1, 1, nb, HEAD_DIM), lambda bb, g: (bb, g, 0, 0)),
        out_shape=jax.ShapeDtypeStruct((b, N_NSA_KV, nb, HEAD_DIM), out_dtype),
        compiler_params=pltpu.CompilerParams(
            dimension_semantics=("parallel", "parallel"), vmem_limit_bytes=VMEM_LIMIT),
        name="compress_k" if normalize else "compress_v",
    )(src, pos, w1cat, w2, gain.reshape(1, HEAD_DIM))


def _nsa_kernel(q_ref, kc_ref, vc_ref, ks_ref, vs_ref, kw_ref, vw_ref, gate_ref, gain_ref,
                o_ref, kaug_ref, vaug_ref, vwaug_ref, qaug_ref, m_ref, acc_ref, mix_ref, sc_ref, *, tq):
    i = pl.program_id(2)
    r = NSA_GROUP
    q0 = i * tq
    s_len = ks_ref.shape[2]
    ncb = kc_ref.shape[2]
    n_slc = s_len // SLC_BLOCK
    nsr = -(-n_slc // 8) * 8
    heads = range(r)

    @pl.when(i == 0)
    def _():
        lane = lax.broadcasted_iota(jnp.int32, (s_len, LANES), 1)
        kblk = lax.broadcasted_iota(jnp.int32, (s_len, LANES), 0) // SLC_BLOCK
        kaug_ref[:, :HEAD_DIM] = ks_ref[0, 0]
        kaug_ref[:, HEAD_DIM:] = jnp.where(lane == kblk, MASK_SCORE, 0.0).astype(BF16)
        ones = jnp.ones((s_len, LANES), BF16)
        vaug_ref[:, :HEAD_DIM] = vs_ref[0, 0]
        vaug_ref[:, HEAD_DIM:] = ones
        vwaug_ref[:, :HEAD_DIM] = vw_ref[0, 0]
        vwaug_ref[:, HEAD_DIM:] = ones

    gt = 1.0 / (1.0 + jnp.exp(-gate_ref[0, 0].astype(F32)))

    def gate(h, c):
        return gt[:, 3 * h + c:3 * h + c + 1]

    kc = kc_ref[0, 0].astype(BF16)
    vc = vc_ref[0, 0]
    cend = lax.broadcasted_iota(jnp.int32, (tq, ncb), 1) * CMP_STRIDE + (CMP_BLOCK - 1)
    cvalid = cend <= q0 + lax.broadcasted_iota(jnp.int32, (tq, ncb), 0)
    imp = jnp.zeros((tq, ncb), F32)
    for h in heads:
        s = jnp.where(cvalid, _nt(q_ref[0, h], kc), NEG_INF)
        e = jnp.where(cvalid, jnp.exp2(s - jnp.max(s, axis=-1, keepdims=True)), 0.0)
        den = jnp.sum(e, axis=-1, keepdims=True)
        pc = e * (1.0 / jnp.where(den > 0.0, den, 1.0))
        imp = imp + pc
        mix_ref[h * tq:(h + 1) * tq, :] = gate(h, 0) * _mm(pc.astype(BF16), vc)

    wwidth = WINDOW + tq
    wstart = jnp.maximum(q0 - WINDOW, 0)
    wrows = pl.ds(pl.multiple_of(wstart, tq), wwidth)
    kwin = kw_ref[0, 0, wrows, :]
    vwin = vwaug_ref[wrows, :]
    ahead = (lax.broadcasted_iota(jnp.int32, (tq, wwidth), 1)
             - lax.broadcasted_iota(jnp.int32, (tq, wwidth), 0)) + (wstart - q0)
    wmask = (ahead <= 0) & (ahead > -WINDOW)
    sw = [jnp.where(wmask, _nt(q_ref[0, h], kwin), NEG_INF) for h in heads]
    pw = [jnp.exp2(x - jnp.max(x, axis=-1, keepdims=True)).astype(BF16) for x in sw]
    ow = [_mm(x, vwin) for x in pw]
    for h in heads:
        mix_ref[h * tq:(h + 1) * tq, :] += gate(h, 2) * (ow[h][:, :HEAD_DIM] / ow[h][:, HEAD_DIM:])

    nsb = nsr
    imp_hi = imp.astype(BF16)
    imp_lo = (imp - imp_hi.astype(F32)).astype(BF16)
    brow = lax.broadcasted_iota(jnp.int32, (nsb, ncb), 0)
    ccol = lax.broadcasted_iota(jnp.int32, (nsb, ncb), 1)
    ratio = SLC_BLOCK // CMP_STRIDE
    span = CMP_BLOCK // CMP_STRIDE
    cover = ((ccol >= ratio * brow - (span - 1)) & (ccol <= ratio * brow + ratio - 1)
             & (ccol < ncb - 1)).astype(BF16)
    p_slc = _nt(cover, imp_hi) + _nt(cover, imp_lo)
    bidx = lax.broadcasted_iota(jnp.int32, (nsb, tq), 0)
    t_lane = q0 + lax.broadcasted_iota(jnp.int32, (nsb, tq), 1)
    cur = t_lane // SLC_BLOCK
    forced = (bidx == 0) | (bidx == cur) | (bidx == cur - 1)
    score = jnp.where(bidx <= cur, jnp.where(forced, FORCE_SCORE, p_slc), NEG_INF)
    score = jnp.where(bidx < n_slc, score, -3e38)
    sc_ref[...] = score
    ranks = []
    for c0 in range(0, nsb, 8):
        mine = score[c0:c0 + 8, :]
        brow8 = c0 + lax.broadcasted_iota(jnp.int32, (8, tq), 0)
        rank = jnp.zeros((8, tq), F32)
        for bp in range(n_slc):
            other = sc_ref[bp:bp + 1, :]
            ge = jnp.where(other >= mine, 1.0, 0.0)
            gt_ = jnp.where(other > mine, 1.0, 0.0)
            if bp < c0:
                rank = rank + ge
            elif bp >= c0 + 8:
                rank = rank + gt_
            else:
                rank = rank + jnp.where(brow8 > bp, ge, gt_)
        ranks.append(rank)
    rank = jnp.concatenate(ranks, axis=0)
    sel_t = jnp.where((rank < float(min(N_SELECT, n_slc))) & (bidx <= cur) & (bidx < n_slc), 0.0, -1.0)
    if nsb < LANES:
        sel_t = jnp.concatenate([sel_t, jnp.full((LANES - nsb, tq), -1.0, F32)], axis=0)
    selm1 = jnp.transpose(sel_t).astype(BF16)
    for h in heads:
        qaug_ref[h * tq:(h + 1) * tq, :HEAD_DIM] = q_ref[0, h]
        qaug_ref[h * tq:(h + 1) * tq, HEAD_DIM:] = selm1

    def init():
        m_ref[...] = jnp.full(m_ref.shape, NEG_INF, F32)
        acc_ref[...] = jnp.zeros(acc_ref.shape, F32)

    def flash(q_of, k, v, mask):
        width = k.shape[0]
        s = [_nt(q_of(h), k) for h in heads]
        p, alpha = [], []
        for h in heads:
            rows = pl.ds(h * tq, tq)
            sh = s[h] if mask is None else jnp.where(mask, s[h], NEG_INF)
            m_prev = m_ref[rows, :]
            m_next = jnp.maximum(m_prev, jnp.max(sh, axis=-1, keepdims=True))
            p.append(jnp.exp2(sh - jnp.tile(m_next, (1, width // LANES))).astype(BF16))
            alpha.append(jnp.exp2(m_prev - m_next))
            m_ref[rows, :] = m_next
        for h in heads:
            rows = pl.ds(h * tq, tq)
            acc_ref[rows, :] = jnp.tile(alpha[h], (1, 2)) * acc_ref[rows, :] + _mm(p[h], v)

    def result(h):
        rows = pl.ds(h * tq, tq)
        return acc_ref[rows, :HEAD_DIM] / acc_ref[rows, HEAD_DIM:]

    init()

    def q_slc(h):
        return qaug_ref[h * tq:(h + 1) * tq, :]

    def slc_body(j, carry):
        rows = pl.ds(pl.multiple_of(j * (SLC_STEP * tq), SLC_STEP * tq), SLC_STEP * tq)
        flash(q_slc, kaug_ref[rows, :], vaug_ref[rows, :], None)
        return carry

    lax.fori_loop(0, i // SLC_STEP, slc_body, 0)

    for left in range(SLC_STEP):
        @pl.when(i % SLC_STEP == left)
        def _(left=left):
            width = (left + 1) * tq
            rows = pl.ds(pl.multiple_of((i - left) * tq, tq), width)
            ahead = (lax.broadcasted_iota(jnp.int32, (tq, width), 1)
                     - lax.broadcasted_iota(jnp.int32, (tq, width), 0)) - left * tq
            flash(q_slc, kaug_ref[rows, :], vaug_ref[rows, :], ahead <= 0)

    for h in heads:
        merged = mix_ref[h * tq:(h + 1) * tq, :] + gate(h, 1) * result(h)
        o_ref[0, :, h * HEAD_DIM:(h + 1) * HEAD_DIM] = _rms(merged, gain_ref[0, h:h + 1, :]).astype(BF16)


def _nsa(p, kc, vc, gains, tq):
    b, _, s, _ = p.shape
    g, r = N_NSA_KV, NSA_GROUP
    ncb = kc.shape[2]
    nsr = -(-(s // SLC_BLOCK) // 8) * 8

    def kv_spec(first):
        return pl.BlockSpec((1, 1, s, LANES), lambda bb, gg, i: (bb, first + gg, 0, 0))

    return pl.pallas_call(
        functools.partial(_nsa_kernel, tq=tq),
        grid=(b, g, s // tq),
        in_specs=[
            pl.BlockSpec((1, r, tq, LANES), lambda bb, gg, i: (bb, gg, i, 0)),
            pl.BlockSpec((1, 1, ncb, LANES), lambda bb, gg, i: (bb, gg, 0, 0)),
            pl.BlockSpec((1, 1, ncb, LANES), lambda bb, gg, i: (bb, gg, 0, 0)),
            kv_spec(BLK_KS), kv_spec(BLK_VS), kv_spec(BLK_KW), kv_spec(BLK_VW),
            pl.BlockSpec((1, 1, tq, LANES), lambda bb, gg, i: (bb, BLK_GATE + gg, i, 0)),
            pl.BlockSpec((1, r, HEAD_DIM), lambda bb, gg, i: (gg, 0, 0)),
        ],
        out_specs=pl.BlockSpec((1, tq, r * HEAD_DIM), lambda bb, gg, i: (bb, i, gg)),
        out_shape=jax.ShapeDtypeStruct((b, s, N_NSA_HEADS * HEAD_DIM), BF16),
        scratch_shapes=[
            pltpu.VMEM((s, 2 * LANES), BF16),
            pltpu.VMEM((s, 2 * LANES), BF16),
            pltpu.VMEM((s, 2 * LANES), BF16),
            pltpu.VMEM((r * tq, 2 * LANES), BF16),
            pltpu.VMEM((r * tq, LANES), F32),
            pltpu.VMEM((r * tq, 2 * LANES), F32),
            pltpu.VMEM((r * tq, HEAD_DIM), F32),
            pltpu.VMEM((nsr, tq), F32),
        ],
        compiler_params=pltpu.CompilerParams(
            dimension_semantics=("parallel", "parallel", "arbitrary"), vmem_limit_bytes=VMEM_LIMIT),
        name="nsa",
    )(p, kc, vc, p, p, p, p, p, gains)


def _sb_kernel(q_ref, k_ref, v_ref, tri_ref, gain_ref, o_ref, acc_ref, c_ref, *, tk, parts):
    i = pl.program_id(2)
    acc_ref[...] = jnp.zeros(acc_ref.shape, F32)
    c_ref[...] = jnp.zeros(c_ref.shape, F32)
    tri = tri_ref[...]
    row = lax.broadcasted_iota(jnp.int32, (tk, tk), 0)
    col = lax.broadcasted_iota(jnp.int32, (tk, tk), 1)
    strict = col < row

    def tiles(k, v, work):
        rows = [pl.ds(part * tk, tk) for part, _ in work]
        zn = [_nt(q_ref[0, 0, r, :], k) for r in rows]
        lk = []
        for z, (_, masked) in zip(zn, work):
            x = jnp.minimum(z, 0.0) - jnp.log2(1.0 + jnp.exp2(-jnp.abs(z)))
            lk.append(jnp.where(strict, x, 0.0) if masked else x)
        hi = [x.astype(BF16) for x in lk]
        lo = [(x - h.astype(F32)).astype(BF16) for x, h in zip(lk, hi)]
        incl = [_mm(jnp.concatenate([h, l], axis=1), tri) for h, l in zip(hi, lo)]
        a = []
        for n, z, r, (_, masked) in zip(incl, zn, rows, work):
            x = jnp.exp2(n + jnp.tile(c_ref[r, :], (1, tk // LANES)) - z)
            a.append((jnp.where(strict, x, 0.0) if masked else x).astype(BF16))
        for x, r in zip(a, rows):
            acc_ref[r, :] += _mm(x, v)
        for x, r in zip(lk, rows):
            c_ref[r, :] += jnp.sum(x, axis=-1, keepdims=True)

    def kv(j):
        k0 = pl.multiple_of(j * tk, tk)
        return k_ref[0, 0, pl.ds(k0, tk), :], v_ref[0, 0, pl.ds(k0, tk), :]

    for d in reversed(range(parts)):
        k, v = kv(parts * i + d)
        tiles(k, v, [(d, True)] + [(p, False) for p in range(d + 1, parts)])

    def body(jj, carry):
        k, v = kv(parts * i - 1 - jj)
        tiles(k, v, [(p, False) for p in range(parts)])
        return carry

    lax.fori_loop(0, parts * i, body, 0)
    o_ref[0] = _rms(acc_ref[...], gain_ref[0]).astype(BF16)


def _sb(p, tri, gains, tk, parts):
    b, _, s, _ = p.shape
    tq = parts * tk
    return pl.pallas_call(
        functools.partial(_sb_kernel, tk=tk, parts=parts),
        grid=(b, N_SB_HEADS, s // tq),
        in_specs=[
            pl.BlockSpec((1, 1, tq, LANES), lambda bb, h, i: (bb, BLK_QS + h, i, 0)),
            pl.BlockSpec((1, 1, s, LANES), lambda bb, h, i: (bb, BLK_KSB + h, 0, 0)),
            pl.BlockSpec((1, 1, s, LANES), lambda bb, h, i: (bb, BLK_VSB + h, 0, 0)),
            pl.BlockSpec((2 * tk, tk), lambda bb, h, i: (0, 0)),
            pl.BlockSpec((1, 1, HEAD_DIM), lambda bb, h, i: (h, 0, 0)),
        ],
        out_specs=pl.BlockSpec((1, tq, HEAD_DIM), lambda bb, h, i: (bb, i, h)),
        out_shape=jax.ShapeDtypeStruct((b, s, N_SB_HEADS * HEAD_DIM), BF16),
        scratch_shapes=[
            pltpu.VMEM((tq, HEAD_DIM), F32),
            pltpu.VMEM((tq, LANES), F32),
        ],
        compiler_params=pltpu.CompilerParams(
            dimension_semantics=("parallel", "parallel", "arbitrary"), vmem_limit_bytes=VMEM_LIMIT),
        name="sb",
    )(p, p, p, tri, gains)


def _out_kernel(x_ref, ya_ref, yb_ref, wa_ref, wb_ref, o_ref):
    o_ref[...] = x_ref[...] + _mm(ya_ref[...], wa_ref[...]) + _mm(yb_ref[...], wb_ref[...])


def _out_proj(x2d, ya, yb, wa, wb, tm):
    t, d = x2d.shape
    ka, kb = ya.shape[1], yb.shape[1]
    return pl.pallas_call(
        _out_kernel,
        grid=(t // tm,),
        in_specs=[
            pl.BlockSpec((tm, d), lambda i: (i, 0)),
            pl.BlockSpec((tm, ka), lambda i: (i, 0)),
            pl.BlockSpec((tm, kb), lambda i: (i, 0)),
            pl.BlockSpec((ka, d), lambda i: (0, 0)),
            pl.BlockSpec((kb, d), lambda i: (0, 0)),
        ],
        out_specs=pl.BlockSpec((tm, d), lambda i: (i, 0)),
        out_shape=jax.ShapeDtypeStruct((t, d), F32),
        compiler_params=pltpu.CompilerParams(
            dimension_semantics=("parallel",), vmem_limit_bytes=VMEM_LIMIT),
        name="out_proj",
    )(x2d, ya, yb, wa, wb)


def _tile(n, pref):
    t = min(n, pref)
    while n % t:
        t //= 2
    return t


def _pack_w_in(w_in):
    d = w_in.shape[0]
    hd = HEAD_DIM
    kv = N_NSA_KV * hd
    nq = N_NSA_HEADS * hd
    o_gate = nq + 6 * kv
    o_sb = o_gate + N_NSA_HEADS * 3
    gate = w_in[:, o_gate:o_sb].reshape(d, N_NSA_KV, NSA_GROUP * 3)
    gate = jnp.pad(gate, ((0, 0), (0, 0), (0, hd - NSA_GROUP * 3))).reshape(d, N_NSA_KV * hd)
    pad = jnp.zeros((d, (BLK_QS - BLK_GATE - N_NSA_KV) * hd), w_in.dtype)
    return jnp.concatenate([w_in[:, :o_gate], gate, pad, w_in[:, o_sb:]], axis=1).astype(BF16)


def kernel(x, positions, ffn1_norm, ffn1_w_gate, ffn1_w_up, ffn1_w_down, mix_norm, w_in,
           nsa_q_norm, nsa_k_cmp_norm, nsa_k_slc_norm, nsa_k_win_norm,
           cmp_k_pos, cmp_k_w1, cmp_k_w2, cmp_v_pos, cmp_v_w1, cmp_v_w2,
           nsa_out_norm, sb_out_norm, w_out, ffn2_norm, ffn2_w_gate, ffn2_w_up, ffn2_w_down):
    b, s, d = x.shape
    depth = w_in.shape[0]
    hd = HEAD_DIM
    hid = cmp_k_w2.shape[1]
    t = b * s
    tm = _tile(t, 512)
    tm_ffn = _tile(t, 1024)
    tf = _tile(ffn1_w_gate.shape[2], 512)
    tq_nsa = 256
    tk_sb, parts_sb = 256, 8
    assert s % (tk_sb * parts_sb) == 0 and s // SLC_BLOCK <= LANES and WINDOW % tq_nsa == 0 and s % tq_nsa == 0 and s >= WINDOW + tq_nsa

    inv_freq = ROPE_THETA ** (-jnp.arange(0, hd, 2, dtype=F32) / hd)
    ang = positions.astype(F32)[..., None] * inv_freq
    cosf = jnp.concatenate([jnp.cos(ang), jnp.cos(ang)], axis=-1)
    sinf = jnp.concatenate([-jnp.sin(ang), jnp.sin(ang)], axis=-1)

    idx = jnp.arange(tk_sb, dtype=jnp.int32)
    tri = (idx[:, None] >= idx[None, :]).astype(BF16)
    tri = jnp.concatenate([tri, tri], axis=0)

    def w1cat(w1):
        w = w1.reshape(CMP_BLOCK, hd, hid)
        return jnp.concatenate([w[:CMP_STRIDE], w[CMP_STRIDE:]], axis=-1).astype(BF16)

    for l in range(depth):
        x2 = _ffn(x.reshape(t, d), ffn1_norm[l], ffn1_w_gate[l].astype(BF16), ffn1_w_up[l].astype(BF16),
                  ffn1_w_down[l].astype(BF16), tm_ffn, tf)

        head_gains = jnp.zeros((8, hd), F32)
        head_gains = head_gains.at[0].set(nsa_q_norm[l] * (hd ** -0.5 * LOG2E)).at[1].set(nsa_k_slc_norm[l]).at[2].set(nsa_k_win_norm[l])
        p, csrc = _in_proj(x2.reshape(b, s, d), mix_norm[l], _pack_w_in(w_in[l]), cosf, sinf, head_gains, tm=_tile(s, 1024))

        kc = _compress(csrc, 0, cmp_k_pos[l], w1cat(cmp_k_w1[l]), cmp_k_w2[l].astype(BF16),
                       nsa_k_cmp_norm[l], True, F32)
        vc = _compress(csrc, N_NSA_KV, cmp_v_pos[l], w1cat(cmp_v_w1[l]), cmp_v_w2[l].astype(BF16),
                       jnp.ones((hd,), F32), False, BF16)

        y_nsa = _nsa(p, kc, vc, nsa_out_norm[l].reshape(N_NSA_KV, NSA_GROUP, hd), tq_nsa)
        y_sb = _sb(p, tri, sb_out_norm[l].reshape(N_SB_HEADS, 1, hd), tk_sb, parts_sb)

        nmix = N_NSA_HEADS * hd
        x3 = _out_proj(x2, y_nsa.reshape(t, nmix), y_sb.reshape(t, N_SB_HEADS * hd),
                       w_out[l, :nmix].astype(BF16), w_out[l, nmix:].astype(BF16), tm)

        x = _ffn(x3, ffn2_norm[l], ffn2_w_gate[l].astype(BF16), ffn2_w_up[l].astype(BF16),
                 ffn2_w_down[l].astype(BF16), tm_ffn, tf).reshape(b, s, d)
    return x
```

```python
import functools
from typing import NamedTuple

import jax
import jax.numpy as jnp
from jax import lax
from jax.experimental import pallas as pl
from jax.experimental.pallas import tpu as pltpu

HEAD_DIM = 128
N_NSA_HEADS = 8
N_NSA_KV = 2
NSA_GROUP = N_NSA_HEADS // N_NSA_KV
N_SB_HEADS = 8
CMP_BLOCK = 32
CMP_STRIDE = 16
SLC_BLOCK = 64
N_SELECT = 16
WINDOW = 512
ROPE_THETA = 10000.0
EPS = 1e-6
NEG_INF = -1e30
FORCE_SCORE = 1e9
LOG2E = 1.4426950408889634
MASK_SCORE = 1e30

LANES = 128
MXU_TILE = 256
VMEM_LIMIT = 56 * 1024 * 1024

BLK_Q = 0
BLK_KC, BLK_VC = 8, 10
BLK_KS, BLK_VS = 12, 14
BLK_KW, BLK_VW = 16, 18
BLK_GATE = 20
BLK_QS, BLK_KSB, BLK_VSB = 24, 32, 40
N_BLK = 48
PROJ_NB = 8
SLC_STEP = 4

BF16 = jnp.bfloat16
F32 = jnp.float32


def _nt(a, b):
    return lax.dot_general(a, b, (((1,), (1,)), ((), ())), preferred_element_type=F32)


def _mm(a, b):
    return jnp.dot(a, b, preferred_element_type=F32)


def _rms(v, g):
    return v * lax.rsqrt(jnp.mean(v * v, axis=-1, keepdims=True) + EPS) * g


def _ffn_kernel(x_ref, g_ref, wg_ref, wu_ref, wd_ref, o_ref, xn_ref):
    j = pl.program_id(1)

    @pl.when(j == 0)
    def _():
        xn_ref[...] = _rms(x_ref[...], g_ref[...]).astype(BF16)
        o_ref[...] = jnp.zeros_like(o_ref)

    xn = xn_ref[...]
    h = _mm(xn, wg_ref[...])
    u = _mm(xn, wu_ref[...])
    a = (h * (1.0 / (1.0 + jnp.exp(-h)))) * u
    o_ref[...] += _mm(a.astype(BF16), wd_ref[...])

    @pl.when(j == pl.num_programs(1) - 1)
    def _():
        o_ref[...] = x_ref[...] + 0.5 * o_ref[...]


def _ffn(x2d, gain, wg, wu, wd, tm, tf):
    t, d = x2d.shape
    ff = wg.shape[1]
    return pl.pallas_call(
        _ffn_kernel,
        grid=(t // tm, ff // tf),
        in_specs=[
            pl.BlockSpec((tm, d), lambda i, j: (i, 0)),
            pl.BlockSpec((1, d), lambda i, j: (0, 0)),
            pl.BlockSpec((d, tf), lambda i, j: (0, j)),
            pl.BlockSpec((d, tf), lambda i, j: (0, j)),
            pl.BlockSpec((tf, d), lambda i, j: (j, 0)),
        ],
        out_specs=pl.BlockSpec((tm, d), lambda i, j: (i, 0)),
        out_shape=jax.ShapeDtypeStruct((t, d), F32),
        scratch_shapes=[pltpu.VMEM((tm, d), BF16)],
        compiler_params=pltpu.CompilerParams(
            dimension_semantics=("parallel", "arbitrary"), vmem_limit_bytes=VMEM_LIMIT),
        name="ffn",
    )(x2d, gain.reshape(1, d), wg, wu, wd)


def _proj_kernel(x_ref, g_ref, w_ref, cos_ref, sin_ref, hg_ref, p_ref, c_ref, xn_ref):
    j = pl.program_id(2)
    scale = HEAD_DIM ** -0.5

    @pl.when(j == 0)
    def _():
        xn_ref[...] = _rms(x_ref[0], g_ref[...]).astype(BF16)

    acc = _mm(xn_ref[...], w_ref[...])

    def blk(c):
        return acc[:, c * LANES:(c + 1) * LANES]

    def rope(v):
        return v * cos_ref[0] + pltpu.roll(v, HEAD_DIM // 2, 1) * sin_ref[0]

    def put(c, v):
        p_ref[0, c] = v.astype(BF16)

    def finish(b, c):
        v = blk(c)
        if b < BLK_KC:
            put(c, rope(_rms(v, hg_ref[0:1, :])))
        elif b < BLK_KS:
            v = rope(v) if b < BLK_VC else v
            put(c, v)
            c_ref[0, b - BLK_KC] = v
        elif BLK_KS <= b < BLK_VS:
            put(c, rope(_rms(v, hg_ref[1:2, :])))
        elif BLK_KW <= b < BLK_VW:
            put(c, rope(_rms(v, hg_ref[2:3, :])))
        elif BLK_QS <= b < BLK_KSB:
            put(c, v * (-scale * LOG2E))
        else:
            put(c, v)

    for step in range(N_BLK // PROJ_NB):
        @pl.when(j == step)
        def _(step=step):
            for c in range(PROJ_NB):
                finish(step * PROJ_NB + c, c)


def _in_proj(x, gain, w, cosf, sinf, head_gains, tm):
    b, s, d = x.shape
    tn = PROJ_NB * LANES
    return pl.pallas_call(
        _proj_kernel,
        grid=(b, s // tm, N_BLK // PROJ_NB),
        in_specs=[
            pl.BlockSpec((1, tm, d), lambda bb, i, j: (bb, i, 0)),
            pl.BlockSpec((1, d), lambda bb, i, j: (0, 0)),
            pl.BlockSpec((d, tn), lambda bb, i, j: (0, j)),
            pl.BlockSpec((1, tm, LANES), lambda bb, i, j: (bb, i, 0)),
            pl.BlockSpec((1, tm, LANES), lambda bb, i, j: (bb, i, 0)),
            pl.BlockSpec((8, LANES), lambda bb, i, j: (0, 0)),
        ],
        out_specs=[
            pl.BlockSpec((1, PROJ_NB, tm, LANES), lambda bb, i, j: (bb, j, i, 0)),
            pl.BlockSpec((1, 4, tm, LANES), lambda bb, i, j: (bb, 0, i, 0)),
        ],
        out_shape=[
            jax.ShapeDtypeStruct((b, N_BLK, s, LANES), BF16),
            jax.ShapeDtypeStruct((b, 4, s, LANES), F32),
        ],
        scratch_shapes=[pltpu.VMEM((tm, d), BF16)],
        compiler_params=pltpu.CompilerParams(
            dimension_semantics=("parallel", "parallel", "arbitrary"), vmem_limit_bytes=VMEM_LIMIT),
        name="in_proj",
    )(x, gain.reshape(1, d), w, cosf, sinf, head_gains)


def _compress_kernel(x_ref, pos_ref, w1_ref, w2_ref, g_ref, o_ref, *, normalize):
    nb = o_ref.shape[2]
    hid = w2_ref.shape[0]
    acc = jnp.zeros((nb, 2 * hid), F32)
    bias = jnp.zeros((8, hid), F32)
    for l in range(CMP_STRIDE):
        rows = x_ref[0, 0, pl.ds(l, nb, stride=CMP_STRIDE), :]
        acc = acc + _mm(rows.astype(BF16), w1_ref[l])
        p_lo = jnp.broadcast_to(pos_ref[l:l + 1, :], (8, HEAD_DIM)).astype(BF16)
        p_hi = jnp.broadcast_to(pos_ref[l + CMP_STRIDE:l + CMP_STRIDE + 1, :], (8, HEAD_DIM)).astype(BF16)
        bias = bias + _mm(p_lo, w1_ref[l, :, :hid]) + _mm(p_hi, w1_ref[l, :, hid:])
    hidden = acc[:, :hid] + pltpu.roll(acc[:, hid:], nb - 1, 0) + bias[0:1, :]
    act = 0.5 * hidden * (1.0 + jnp.tanh(0.7978845608028654 * (hidden + 0.044715 * hidden * hidden * hidden)))
    out = _mm(act.astype(BF16), w2_ref[...])
    if normalize:
        out = _rms(out, g_ref[...])
    row = lax.broadcasted_iota(jnp.int32, out.shape, 0)
    o_ref[0, 0] = jnp.where(row < nb - 1, out, 0.0).astype(o_ref.dtype)


def _compress(src, first_blk, pos, w1cat, w2, gain, normalize, out_dtype):
    b, _, s, _ = src.shape
    nb = s // CMP_STRIDE
    hid = w2.shape[0]
    return pl.pallas_call(
        functools.partial(_compress_kernel, normalize=normalize),
        grid=(b, N_NSA_KV),
        in_specs=[
            pl.BlockSpec((1, 1, s, LANES), lambda bb, g: (bb, first_blk + g, 0, 0)),
            pl.BlockSpec((CMP_BLOCK, HEAD_DIM), lambda bb, g: (0, 0)),
            pl.BlockSpec((CMP_STRIDE, HEAD_DIM, 2 * hid), lambda bb, g: (0, 0, 0)),
            pl.BlockSpec((hid, HEAD_DIM), lambda bb, g: (0, 0)),
            pl.BlockSpec((1, HEAD_DIM), lambda bb, g: (0, 0)),
        ],
        out_specs=pl.BlockSpec((1, 1, nb, HEAD_DIM), lambda bb, g: (bb, g, 0, 0)),
        out_shape=jax.ShapeDtypeStruct((b, N_NSA_KV, nb, HEAD_DIM), out_dtype),
        compiler_params=pltpu.CompilerParams(
            dimension_semantics=("parallel", "parallel"), vmem_limit_bytes=VMEM_LIMIT),
        name="compress_k" if normalize else "compress_v",
    )(src, pos, w1cat, w2, gain.reshape(1, HEAD_DIM))


def _nsa_kernel(q_ref, kc_ref, vc_ref, ks_ref, vs_ref, kw_ref, vw_ref, gate_ref, gain_ref,
                o_ref, kaug_ref, vaug_ref, vwaug_ref, qaug_ref, m_ref, acc_ref, mix_ref, sc_ref, *, tq):
    i = pl.program_id(2)
    r = NSA_GROUP
    q0 = i * tq
    s_len = ks_ref.shape[2]
    ncb = kc_ref.shape[2]
    n_slc = s_len // SLC_BLOCK
    nsr = -(-n_slc // 8) * 8
    heads = range(r)

    @pl.when(i == 0)
    def _():
        lane = lax.broadcasted_iota(jnp.int32, (s_len, LANES), 1)
        kblk = lax.broadcasted_iota(jnp.int32, (s_len, LANES), 0) // SLC_BLOCK
        kaug_ref[:, :HEAD_DIM] = ks_ref[0, 0]
        kaug_ref[:, HEAD_DIM:] = jnp.where(lane == kblk, MASK_SCORE, 0.0).astype(BF16)
        ones = jnp.ones((s_len, LANES), BF16)
        vaug_ref[:, :HEAD_DIM] = vs_ref[0, 0]
        vaug_ref[:, HEAD_DIM:] = ones
        vwaug_ref[:, :HEAD_DIM] = vw_ref[0, 0]
        vwaug_ref[:, HEAD_DIM:] = ones

    gt = 1.0 / (1.0 + jnp.exp(-gate_ref[0, 0].astype(F32)))

    def gate(h, c):
        return gt[:, 3 * h + c:3 * h + c + 1]

    kc = kc_ref[0, 0].astype(BF16)
    vc = vc_ref[0, 0]
    cend = lax.broadcasted_iota(jnp.int32, (tq, ncb), 1) * CMP_STRIDE + (CMP_BLOCK - 1)
    cvalid = cend <= q0 + lax.broadcasted_iota(jnp.int32, (tq, ncb), 0)
    imp = jnp.zeros((tq, ncb), F32)
    for h in heads:
        s = jnp.where(cvalid, _nt(q_ref[0, h], kc), NEG_INF)
        e = jnp.where(cvalid, jnp.exp2(s - jnp.max(s, axis=-1, keepdims=True)), 0.0)
        den = jnp.sum(e, axis=-1, keepdims=True)
        pc = e * (1.0 / jnp.where(den > 0.0, den, 1.0))
        imp = imp + pc
        mix_ref[h * tq:(h + 1) * tq, :] = gate(h, 0) * _mm(pc.astype(BF16), vc)

    wwidth = WINDOW + tq
    wstart = jnp.maximum(q0 - WINDOW, 0)
    wrows = pl.ds(pl.multiple_of(wstart, tq), wwidth)
    kwin = kw_ref[0, 0, wrows, :]
    vwin = vwaug_ref[wrows, :]
    ahead = (lax.broadcasted_iota(jnp.int32, (tq, wwidth), 1)
             - lax.broadcasted_iota(jnp.int32, (tq, wwidth), 0)) + (wstart - q0)
    wmask = (ahead <= 0) & (ahead > -WINDOW)
    sw = [jnp.where(wmask, _nt(q_ref[0, h], kwin), NEG_INF) for h in heads]
    pw = [jnp.exp2(x - jnp.max(x, axis=-1, keepdims=True)).astype(BF16) for x in sw]
    ow = [_mm(x, vwin) for x in pw]
    for h in heads:
        mix_ref[h * tq:(h + 1) * tq, :] += gate(h, 2) * (ow[h][:, :HEAD_DIM] / ow[h][:, HEAD_DIM:])

    nsb = nsr
    imp_hi = imp.astype(BF16)
    imp_lo = (imp - imp_hi.astype(F32)).astype(BF16)
    brow = lax.broadcasted_iota(jnp.int32, (nsb, ncb), 0)
    ccol = lax.broadcasted_iota(jnp.int32, (nsb, ncb), 1)
    ratio = SLC_BLOCK // CMP_STRIDE
    span = CMP_BLOCK // CMP_STRIDE
    cover = ((ccol >= ratio * brow - (span - 1)) & (ccol <= ratio * brow + ratio - 1)
             & (ccol < ncb - 1)).astype(BF16)
    p_slc = _nt(cover, imp_hi) + _nt(cover, imp_lo)
    bidx = lax.broadcasted_iota(jnp.int32, (nsb, tq), 0)
    t_lane = q0 + lax.broadcasted_iota(jnp.int32, (nsb, tq), 1)
    cur = t_lane // SLC_BLOCK
    forced = (bidx == 0) | (bidx == cur) | (bidx == cur - 1)
    score = jnp.where(bidx <= cur, jnp.where(forced, FORCE_SCORE, p_slc), NEG_INF)
    score = jnp.where(bidx < n_slc, score, -3e38)
    sc_ref[...] = score
    ranks = []
    for c0 in range(0, nsb, 8):
        mine = score[c0:c0 + 8, :]
        brow8 = c0 + lax.broadcasted_iota(jnp.int32, (8, tq), 0)
        rank = jnp.zeros((8, tq), F32)
        for bp in range(n_slc):
            other = sc_ref[bp:bp + 1, :]
            ge = jnp.where(other >= mine, 1.0, 0.0)
            gt_ = jnp.where(other > mine, 1.0, 0.0)
            if bp < c0:
                rank = rank + ge
            elif bp >= c0 + 8:
                rank = rank + gt_
            else:
                rank = rank + jnp.where(brow8 > bp, ge, gt_)
        ranks.append(rank)
    rank = jnp.concatenate(ranks, axis=0)
    sel_t = jnp.where((rank < float(min(N_SELECT, n_slc))) & (bidx <= cur) & (bidx < n_slc), 0.0, -1.0)
    if nsb < LANES:
        sel_t = jnp.concatenate([sel_t, jnp.full((LANES - nsb, tq), -1.0, F32)], axis=0)
    selm1 = jnp.transpose(sel_t).astype(BF16)
    for h in heads:
        qaug_ref[h * tq:(h + 1) * tq, :HEAD_DIM] = q_ref[0, h]
        qaug_ref[h * tq:(h + 1) * tq, HEAD_DIM:] = selm1

    def init():
        m_ref[...] = jnp.full(m_ref.shape, NEG_INF, F32)
        acc_ref[...] = jnp.zeros(acc_ref.shape, F32)

    def flash(q_of, k, v, mask):
        width = k.shape[0]
        s = [_nt(q_of(h), k) for h in heads]
        p, alpha = [], []
        for h in heads:
            rows = pl.ds(h * tq, tq)
            sh = s[h] if mask is None else jnp.where(mask, s[h], NEG_INF)
            m_prev = m_ref[rows, :]
            m_next = jnp.maximum(m_prev, jnp.max(sh, axis=-1, keepdims=True))
            p.append(jnp.exp2(sh - jnp.tile(m_next, (1, width // LANES))).astype(BF16))
            alpha.append(jnp.exp2(m_prev - m_next))
            m_ref[rows, :] = m_next
        for h in heads:
            rows = pl.ds(h * tq, tq)
            acc_ref[rows, :] = jnp.tile(alpha[h], (1, 2)) * acc_ref[rows, :] + _mm(p[h], v)

    def result(h):
        rows = pl.ds(h * tq, tq)
        return acc_ref[rows, :HEAD_DIM] / acc_ref[rows, HEAD_DIM:]

    init()

    def q_slc(h):
        return qaug_ref[h * tq:(h + 1) * tq, :]

    def slc_body(j, carry):
        rows = pl.ds(pl.multiple_of(j * (SLC_STEP * tq), SLC_STEP * tq), SLC_STEP * tq)
        flash(q_slc, kaug_ref[rows, :], vaug_ref[rows, :], None)
        return carry

    lax.fori_loop(0, i // SLC_STEP, slc_body, 0)

    for left in range(SLC_STEP):
        @pl.when(i % SLC_STEP == left)
        def _(left=left):
            width = (left + 1) * tq
            rows = pl.ds(pl.multiple_of((i - left) * tq, tq), width)
            ahead = (lax.broadcasted_iota(jnp.int32, (tq, width), 1)
                     - lax.broadcasted_iota(jnp.int32, (tq, width), 0)) - left * tq
            flash(q_slc, kaug_ref[rows, :], vaug_ref[rows, :], ahead <= 0)

    for h in heads:
        merged = mix_ref[h * tq:(h + 1) * tq, :] + gate(h, 1) * result(h)
        o_ref[0, :, h * HEAD_DIM:(h + 1) * HEAD_DIM] = _rms(merged, gain_ref[0, h:h + 1, :]).astype(BF16)


def _nsa(p, kc, vc, gains, tq):
    b, _, s, _ = p.shape
    g, r = N_NSA_KV, NSA_GROUP
    ncb = kc.shape[2]
    nsr = -(-(s // SLC_BLOCK) // 8) * 8

    def kv_spec(first):
        return pl.BlockSpec((1, 1, s, LANES), lambda bb, gg, i: (bb, first + gg, 0, 0))

    return pl.pallas_call(
        functools.partial(_nsa_kernel, tq=tq),
        grid=(b, g, s // tq),
        in_specs=[
            pl.BlockSpec((1, r, tq, LANES), lambda bb, gg, i: (bb, gg, i, 0)),
            pl.BlockSpec((1, 1, ncb, LANES), lambda bb, gg, i: (bb, gg, 0, 0)),
            pl.BlockSpec((1, 1, ncb, LANES), lambda bb, gg, i: (bb, gg, 0, 0)),
            kv_spec(BLK_KS), kv_spec(BLK_VS), kv_spec(BLK_KW), kv_spec(BLK_VW),
            pl.BlockSpec((1, 1, tq, LANES), lambda bb, gg, i: (bb, BLK_GATE + gg, i, 0)),
            pl.BlockSpec((1, r, HEAD_DIM), lambda bb, gg, i: (gg, 0, 0)),
        ],
        out_specs=pl.BlockSpec((1, tq, r * HEAD_DIM), lambda bb, gg, i: (bb, i, gg)),
        out_shape=jax.ShapeDtypeStruct((b, s, N_NSA_HEADS * HEAD_DIM), BF16),
        scratch_shapes=[
            pltpu.VMEM((s, 2 * LANES), BF16),
            pltpu.VMEM((s, 2 * LANES), BF16),
            pltpu.VMEM((s, 2 * LANES), BF16),
            pltpu.VMEM((r * tq, 2 * LANES), BF16),
            pltpu.VMEM((r * tq, LANES), F32),
            pltpu.VMEM((r * tq, 2 * LANES), F32),
            pltpu.VMEM((r * tq, HEAD_DIM), F32),
            pltpu.VMEM((nsr, tq), F32),
        ],
        compiler_params=pltpu.CompilerParams(
            dimension_semantics=("parallel", "parallel", "arbitrary"), vmem_limit_bytes=VMEM_LIMIT),
        name="nsa",
    )(p, kc, vc, p, p, p, p, p, gains)


def _sb_kernel(q_ref, k_ref, v_ref, tri_ref, gain_ref, o_ref, acc_ref, c_ref, *, tk, parts):
    i = pl.program_id(2)
    acc_ref[...] = jnp.zeros(acc_ref.shape, F32)
    c_ref[...] = jnp.zeros(c_ref.shape, F32)
    tri = tri_ref[...]
    row = lax.broadcasted_iota(jnp.int32, (tk, tk), 0)
    col = lax.broadcasted_iota(jnp.int32, (tk, tk), 1)
    strict = col < row

    def tiles(k, v, work):
        rows = [pl.ds(part * tk, tk) for part, _ in work]
        zn = [_nt(q_ref[0, 0, r, :], k) for r in rows]
        lk = []
        for z, (_, masked) in zip(zn, work):
            x = jnp.minimum(z, 0.0) - jnp.log2(1.0 + jnp.exp2(-jnp.abs(z)))
            lk.append(jnp.where(strict, x, 0.0) if masked else x)
        incl = [_mm(x.astype(BF16), tri) for x in lk]
        a = []
        for n, z, r, (_, masked) in zip(incl, zn, rows, work):
            x = jnp.exp2(n + jnp.tile(c_ref[r, :], (1, tk // LANES)) - z)
            a.append((jnp.where(strict, x, 0.0) if masked else x).astype(BF16))
        for x, r in zip(a, rows):
            acc_ref[r, :] += _mm(x, v)
        for x, r in zip(lk, rows):
            c_ref[r, :] += jnp.sum(x, axis=-1, keepdims=True)

    def kv(j):
        k0 = pl.multiple_of(j * tk, tk)
        return k_ref[0, 0, pl.ds(k0, tk), :], v_ref[0, 0, pl.ds(k0, tk), :]

    for d in reversed(range(parts)):
        k, v = kv(parts * i + d)
        tiles(k, v, [(d, True)] + [(p, False) for p in range(d + 1, parts)])

    def body(jj, carry):
        k, v = kv(parts * i - 1 - jj)
        tiles(k, v, [(p, False) for p in range(parts)])
        return carry

    lax.fori_loop(0, parts * i, body, 0)
    o_ref[0] = _rms(acc_ref[...], gain_ref[0]).astype(BF16)


def _sb(p, tri, gains, tk, parts):
    b, _, s, _ = p.shape
    tq = parts * tk
    return pl.pallas_call(
        functools.partial(_sb_kernel, tk=tk, parts=parts),
        grid=(b, N_SB_HEADS, s // tq),
        in_specs=[
            pl.BlockSpec((1, 1, tq, LANES), lambda bb, h, i: (bb, BLK_QS + h, i, 0)),
            pl.BlockSpec((1, 1, s, LANES), lambda bb, h, i: (bb, BLK_KSB + h, 0, 0)),
            pl.BlockSpec((1, 1, s, LANES), lambda bb, h, i: (bb, BLK_VSB + h, 0, 0)),
            pl.BlockSpec((tk, tk), lambda bb, h, i: (0, 0)),
            pl.BlockSpec((1, 1, HEAD_DIM), lambda bb, h, i: (h, 0, 0)),
        ],
        out_specs=pl.BlockSpec((1, tq, HEAD_DIM), lambda bb, h, i: (bb, i, h)),
        out_shape=jax.ShapeDtypeStruct((b, s, N_SB_HEADS * HEAD_DIM), BF16),
        scratch_shapes=[
            pltpu.VMEM((tq, HEAD_DIM), F32),
            pltpu.VMEM((tq, LANES), F32),
        ],
        compiler_params=pltpu.CompilerParams(
            dimension_semantics=("parallel", "parallel", "arbitrary"), vmem_limit_bytes=VMEM_LIMIT),
        name="sb",
    )(p, p, p, tri, gains)


def _out_kernel(x_ref, ya_ref, yb_ref, wa_ref, wb_ref, o_ref):
    o_ref[...] = x_ref[...] + _mm(ya_ref[...], wa_ref[...]) + _mm(yb_ref[...], wb_ref[...])


def _out_proj(x2d, ya, yb, wa, wb, tm):
    t, d = x2d.shape
    ka, kb = ya.shape[1], yb.shape[1]
    return pl.pallas_call(
        _out_kernel,
        grid=(t // tm,),
        in_specs=[
            pl.BlockSpec((tm, d), lambda i: (i, 0)),
            pl.BlockSpec((tm, ka), lambda i: (i, 0)),
            pl.BlockSpec((tm, kb), lambda i: (i, 0)),
            pl.BlockSpec((ka, d), lambda i: (0, 0)),
            pl.BlockSpec((kb, d), lambda i: (0, 0)),
        ],
        out_specs=pl.BlockSpec((tm, d), lambda i: (i, 0)),
        out_shape=jax.ShapeDtypeStruct((t, d), F32),
        compiler_params=pltpu.CompilerParams(
            dimension_semantics=("parallel",), vmem_limit_bytes=VMEM_LIMIT),
        name="out_proj",
    )(x2d, ya, yb, wa, wb)


def _tile(n, pref):
    t = min(n, pref)
    while n % t:
        t //= 2
    return t


class _Plan(NamedTuple):
    tm_ffn: int
    tf: int
    tm_proj: int
    tm_out: int
    tq_nsa: int
    tk_sb: int
    parts_sb: int


def _plan(b, s, ff):
    plan = _Plan(tm_ffn=_tile(b * s, 1024), tf=_tile(ff, 512), tm_proj=_tile(s, 1024), tm_out=_tile(b * s, 512),
                 tq_nsa=MXU_TILE, tk_sb=MXU_TILE, parts_sb=max(1, min(8, s // MXU_TILE)))
    assert s % (plan.tk_sb * plan.parts_sb) == 0 and s % plan.tq_nsa == 0
    assert s // SLC_BLOCK <= LANES and WINDOW % plan.tq_nsa == 0 and s >= WINDOW + plan.tq_nsa
    return plan


def _pack_w_in(w_in):
    d = w_in.shape[0]
    hd = HEAD_DIM
    kv = N_NSA_KV * hd
    nq = N_NSA_HEADS * hd
    o_gate = nq + 6 * kv
    o_sb = o_gate + N_NSA_HEADS * 3
    gate = w_in[:, o_gate:o_sb].reshape(d, N_NSA_KV, NSA_GROUP * 3)
    gate = jnp.pad(gate, ((0, 0), (0, 0), (0, hd - NSA_GROUP * 3))).reshape(d, N_NSA_KV * hd)
    pad = jnp.zeros((d, (BLK_QS - BLK_GATE - N_NSA_KV) * hd), w_in.dtype)
    return jnp.concatenate([w_in[:, :o_gate], gate, pad, w_in[:, o_sb:]], axis=1).astype(BF16)


def kernel(x, positions, ffn1_norm, ffn1_w_gate, ffn1_w_up, ffn1_w_down, mix_norm, w_in,
           nsa_q_norm, nsa_k_cmp_norm, nsa_k_slc_norm, nsa_k_win_norm,
           cmp_k_pos, cmp_k_w1, cmp_k_w2, cmp_v_pos, cmp_v_w1, cmp_v_w2,
           nsa_out_norm, sb_out_norm, w_out, ffn2_norm, ffn2_w_gate, ffn2_w_up, ffn2_w_down):
    b, s, d = x.shape
    depth = w_in.shape[0]
    hd = HEAD_DIM
    hid = cmp_k_w2.shape[1]
    t = b * s
    plan = _plan(b, s, ffn1_w_gate.shape[2])

    inv_freq = ROPE_THETA ** (-jnp.arange(0, hd, 2, dtype=F32) / hd)
    ang = positions.astype(F32)[..., None] * inv_freq
    cosf = jnp.concatenate([jnp.cos(ang), jnp.cos(ang)], axis=-1)
    sinf = jnp.concatenate([-jnp.sin(ang), jnp.sin(ang)], axis=-1)

    idx = jnp.arange(plan.tk_sb, dtype=jnp.int32)
    tri = (idx[:, None] >= idx[None, :]).astype(BF16)

    def w1cat(w1):
        w = w1.reshape(CMP_BLOCK, hd, hid)
        return jnp.concatenate([w[:CMP_STRIDE], w[CMP_STRIDE:]], axis=-1).astype(BF16)

    for l in range(depth):
        x2 = _ffn(x.reshape(t, d), ffn1_norm[l], ffn1_w_gate[l].astype(BF16), ffn1_w_up[l].astype(BF16),
                  ffn1_w_down[l].astype(BF16), plan.tm_ffn, plan.tf)

        head_gains = jnp.zeros((8, hd), F32)
        head_gains = head_gains.at[0].set(nsa_q_norm[l] * (hd ** -0.5 * LOG2E)).at[1].set(nsa_k_slc_norm[l]).at[2].set(nsa_k_win_norm[l])
        p, csrc = _in_proj(x2.reshape(b, s, d), mix_norm[l], _pack_w_in(w_in[l]), cosf, sinf, head_gains,
                           plan.tm_proj)

        kc = _compress(csrc, 0, cmp_k_pos[l], w1cat(cmp_k_w1[l]), cmp_k_w2[l].astype(BF16),
                       nsa_k_cmp_norm[l], True, F32)
        vc = _compress(csrc, N_NSA_KV, cmp_v_pos[l], w1cat(cmp_v_w1[l]), cmp_v_w2[l].astype(BF16),
                       jnp.ones((hd,), F32), False, BF16)

        y_nsa = _nsa(p, kc, vc, nsa_out_norm[l].reshape(N_NSA_KV, NSA_GROUP, hd), plan.tq_nsa)
        y_sb = _sb(p, tri, sb_out_norm[l].reshape(N_SB_HEADS, 1, hd), plan.tk_sb, plan.parts_sb)

        nmix = N_NSA_HEADS * hd
        x3 = _out_proj(x2, y_nsa.reshape(t, nmix), y_sb.reshape(t, N_SB_HEADS * hd),
                       w_out[l, :nmix].astype(BF16), w_out[l, nmix:].astype(BF16), plan.tm_out)

        x = _ffn(x3, ffn2_norm[l], ffn2_w_gate[l].astype(BF16), ffn2_w_up[l].astype(BF16),
                 ffn2_w_down[l].astype(BF16), plan.tm_ffn, plan.tf).reshape(b, s, d)
    return x
```

```python
import functools
from typing import NamedTuple

import jax
import jax.numpy as jnp
from jax import lax
from jax.experimental import pallas as pl
from jax.experimental.pallas import tpu as pltpu

HEAD_DIM = 128
N_NSA_HEADS = 8
N_NSA_KV = 2
NSA_GROUP = N_NSA_HEADS // N_NSA_KV
N_SB_HEADS = 8
CMP_BLOCK = 32
CMP_STRIDE = 16
SLC_BLOCK = 64
N_SELECT = 16
WINDOW = 512
ROPE_THETA = 10000.0
EPS = 1e-6
NEG_INF = -1e30
FORCE_SCORE = 1e9
LOG2E = 1.4426950408889634
MASK_SCORE = 1e30

LANES = 128
MXU_TILE = 256
VMEM_LIMIT = 56 * 1024 * 1024

BLK_Q = 0
BLK_KC, BLK_VC = 8, 10
BLK_KS, BLK_VS = 12, 14
BLK_KW, BLK_VW = 16, 18
BLK_GATE = 20
BLK_QS, BLK_KSB, BLK_VSB = 24, 32, 40
N_BLK = 48
PROJ_NB = 8
SLC_STEP = 4

BF16 = jnp.bfloat16
F32 = jnp.float32


def _nt(a, b):
    return lax.dot_general(a, b, (((1,), (1,)), ((), ())), preferred_element_type=F32)


def _mm(a, b):
    return jnp.dot(a, b, preferred_element_type=F32)


def _rms(v, g):
    return v * lax.rsqrt(jnp.mean(v * v, axis=-1, keepdims=True) + EPS) * g


def _ffn_kernel(x_ref, g_ref, wg_ref, wu_ref, wd_ref, o_ref, xn_ref):
    j = pl.program_id(1)

    @pl.when(j == 0)
    def _():
        xn_ref[...] = _rms(x_ref[...], g_ref[...]).astype(BF16)
        o_ref[...] = jnp.zeros_like(o_ref)

    xn = xn_ref[...]
    h = _mm(xn, wg_ref[...])
    u = _mm(xn, wu_ref[...])
    a = (h * (1.0 / (1.0 + jnp.exp(-h)))) * u
    o_ref[...] += _mm(a.astype(BF16), wd_ref[...])

    @pl.when(j == pl.num_programs(1) - 1)
    def _():
        o_ref[...] = x_ref[...] + 0.5 * o_ref[...]


def _ffn(x2d, gain, wg, wu, wd, tm, tf):
    t, d = x2d.shape
    ff = wg.shape[1]
    return pl.pallas_call(
        _ffn_kernel,
        grid=(t // tm, ff // tf),
        in_specs=[
            pl.BlockSpec((tm, d), lambda i, j: (i, 0)),
            pl.BlockSpec((1, d), lambda i, j: (0, 0)),
            pl.BlockSpec((d, tf), lambda i, j: (0, j)),
            pl.BlockSpec((d, tf), lambda i, j: (0, j)),
            pl.BlockSpec((tf, d), lambda i, j: (j, 0)),
        ],
        out_specs=pl.BlockSpec((tm, d), lambda i, j: (i, 0)),
        out_shape=jax.ShapeDtypeStruct((t, d), F32),
        scratch_shapes=[pltpu.VMEM((tm, d), BF16)],
        compiler_params=pltpu.CompilerParams(
            dimension_semantics=("parallel", "arbitrary"), vmem_limit_bytes=VMEM_LIMIT),
        name="ffn",
    )(x2d, gain.reshape(1, d), wg, wu, wd)


def _proj_kernel(x_ref, g_ref, w_ref, cos_ref, sin_ref, hg_ref, p_ref, c_ref, xn_ref):
    j = pl.program_id(2)
    scale = HEAD_DIM ** -0.5

    @pl.when(j == 0)
    def _():
        xn_ref[...] = _rms(x_ref[0], g_ref[...]).astype(BF16)

    acc = _mm(xn_ref[...], w_ref[...])

    def blk(c):
        return acc[:, c * LANES:(c + 1) * LANES]

    def rope(v):
        return v * cos_ref[0] + pltpu.roll(v, HEAD_DIM // 2, 1) * sin_ref[0]

    def put(c, v):
        p_ref[0, c] = v.astype(BF16)

    def finish(b, c):
        v = blk(c)
        if b < BLK_KC:
            put(c, rope(_rms(v, hg_ref[0:1, :])))
        elif b < BLK_KS:
            v = rope(v) if b < BLK_VC else v
            put(c, v)
            c_ref[0, b - BLK_KC] = v
        elif BLK_KS <= b < BLK_VS:
            put(c, rope(_rms(v, hg_ref[1:2, :])))
        elif BLK_KW <= b < BLK_VW:
            put(c, rope(_rms(v, hg_ref[2:3, :])))
        elif BLK_QS <= b < BLK_KSB:
            put(c, v * (-scale * LOG2E))
        else:
            put(c, v)

    for step in range(N_BLK // PROJ_NB):
        @pl.when(j == step)
        def _(step=step):
            for c in range(PROJ_NB):
                finish(step * PROJ_NB + c, c)


def _in_proj(x, gain, w, cosf, sinf, head_gains, tm):
    b, s, d = x.shape
    tn = PROJ_NB * LANES
    return pl.pallas_call(
        _proj_kernel,
        grid=(b, s // tm, N_BLK // PROJ_NB),
        in_specs=[
            pl.BlockSpec((1, tm, d), lambda bb, i, j: (bb, i, 0)),
            pl.BlockSpec((1, d), lambda bb, i, j: (0, 0)),
            pl.BlockSpec((d, tn), lambda bb, i, j: (0, j)),
            pl.BlockSpec((1, tm, LANES), lambda bb, i, j: (bb, i, 0)),
            pl.BlockSpec((1, tm, LANES), lambda bb, i, j: (bb, i, 0)),
            pl.BlockSpec((8, LANES), lambda bb, i, j: (0, 0)),
        ],
        out_specs=[
            pl.BlockSpec((1, PROJ_NB, tm, LANES), lambda bb, i, j: (bb, j, i, 0)),
            pl.BlockSpec((1, 4, tm, LANES), lambda bb, i, j: (bb, 0, i, 0)),
        ],
        out_shape=[
            jax.ShapeDtypeStruct((b, N_BLK, s, LANES), BF16),
            jax.ShapeDtypeStruct((b, 4, s, LANES), F32),
        ],
        scratch_shapes=[pltpu.VMEM((tm, d), BF16)],
        compiler_params=pltpu.CompilerParams(
            dimension_semantics=("parallel", "parallel", "arbitrary"), vmem_limit_bytes=VMEM_LIMIT),
        name="in_proj",
    )(x, gain.reshape(1, d), w, cosf, sinf, head_gains)


def _compress_kernel(x_ref, pos_ref, w1_ref, w2_ref, g_ref, o_ref, *, normalize):
    nb = o_ref.shape[2]
    hid = w2_ref.shape[0]
    acc = jnp.zeros((nb, 2 * hid), F32)
    bias = jnp.zeros((8, hid), F32)
    for l in range(CMP_STRIDE):
        rows = x_ref[0, 0, pl.ds(l, nb, stride=CMP_STRIDE), :]
        acc = acc + _mm(rows.astype(BF16), w1_ref[l])
        p_lo = jnp.broadcast_to(pos_ref[l:l + 1, :], (8, HEAD_DIM)).astype(BF16)
        p_hi = jnp.broadcast_to(pos_ref[l + CMP_STRIDE:l + CMP_STRIDE + 1, :], (8, HEAD_DIM)).astype(BF16)
        bias = bias + _mm(p_lo, w1_ref[l, :, :hid]) + _mm(p_hi, w1_ref[l, :, hid:])
    hidden = acc[:, :hid] + pltpu.roll(acc[:, hid:], nb - 1, 0) + bias[0:1, :]
    act = 0.5 * hidden * (1.0 + jnp.tanh(0.7978845608028654 * (hidden + 0.044715 * hidden * hidden * hidden)))
    out = _mm(act.astype(BF16), w2_ref[...])
    if normalize:
        out = _rms(out, g_ref[...])
    row = lax.broadcasted_iota(jnp.int32, out.shape, 0)
    o_ref[0, 0] = jnp.where(row < nb - 1, out, 0.0).astype(o_ref.dtype)


def _compress(src, first_blk, pos, w1cat, w2, gain, normalize, out_dtype):
    b, _, s, _ = src.shape
    nb = s // CMP_STRIDE
    hid = w2.shape[0]
    return pl.pallas_call(
        functools.partial(_compress_kernel, normalize=normalize),
        grid=(b, N_NSA_KV),
        in_specs=[
            pl.BlockSpec((1, 1, s, LANES), lambda bb, g: (bb, first_blk + g, 0, 0)),
            pl.BlockSpec((CMP_BLOCK, HEAD_DIM), lambda bb, g: (0, 0)),
            pl.BlockSpec((CMP_STRIDE, HEAD_DIM, 2 * hid), lambda bb, g: (0, 0, 0)),
            pl.BlockSpec((hid, HEAD_DIM), lambda bb, g: (0, 0)),
            pl.BlockSpec((1, HEAD_DIM), lambda bb, g: (0, 0)),
        ],
        out_specs=pl.BlockSpec((1, 1, nb, HEAD_DIM), lambda bb, g: (bb, g, 0, 0)),
        out_shape=jax.ShapeDtypeStruct((b, N_NSA_KV, nb, HEAD_DIM), out_dtype),
        compiler_params=pltpu.CompilerParams(
            dimension_semantics=("parallel", "parallel"), vmem_limit_bytes=VMEM_LIMIT),
        name="compress_k" if normalize else "compress_v",
    )(src, pos, w1cat, w2, gain.reshape(1, HEAD_DIM))


def _nsa_kernel(q_ref, kc_ref, vc_ref, ks_ref, vs_ref, kw_ref, vw_ref, gate_ref, gain_ref,
                o_ref, kaug_ref, vaug_ref, vwaug_ref, qaug_ref, m_ref, acc_ref, mix_ref, sc_ref, *, tq):
    i = pl.program_id(2)
    r = NSA_GROUP
    q0 = i * tq
    s_len = ks_ref.shape[2]
    ncb = kc_ref.shape[2]
    n_slc = s_len // SLC_BLOCK
    nsr = -(-n_slc // 8) * 8
    heads = range(r)

    @pl.when(i == 0)
    def _():
        lane = lax.broadcasted_iota(jnp.int32, (s_len, LANES), 1)
        kblk = lax.broadcasted_iota(jnp.int32, (s_len, LANES), 0) // SLC_BLOCK
        kaug_ref[:, :HEAD_DIM] = ks_ref[0, 0]
        kaug_ref[:, HEAD_DIM:] = jnp.where(lane == kblk, MASK_SCORE, 0.0).astype(BF16)
        ones = jnp.ones((s_len, LANES), BF16)
        vaug_ref[:, :HEAD_DIM] = vs_ref[0, 0]
        vaug_ref[:, HEAD_DIM:] = ones
        vwaug_ref[:, :HEAD_DIM] = vw_ref[0, 0]
        vwaug_ref[:, HEAD_DIM:] = ones

    gt = 1.0 / (1.0 + jnp.exp(-gate_ref[0, 0].astype(F32)))

    def gate(h, c):
        return gt[:, 3 * h + c:3 * h + c + 1]

    kc = kc_ref[0, 0].astype(BF16)
    vc = vc_ref[0, 0]
    cend = lax.broadcasted_iota(jnp.int32, (tq, ncb), 1) * CMP_STRIDE + (CMP_BLOCK - 1)
    cvalid = cend <= q0 + lax.broadcasted_iota(jnp.int32, (tq, ncb), 0)
    imp = jnp.zeros((tq, ncb), F32)
    for h in heads:
        s = jnp.where(cvalid, _nt(q_ref[0, h], kc), NEG_INF)
        e = jnp.where(cvalid, jnp.exp2(s - jnp.max(s, axis=-1, keepdims=True)), 0.0)
        den = jnp.sum(e, axis=-1, keepdims=True)
        pc = e * (1.0 / jnp.where(den > 0.0, den, 1.0))
        imp = imp + pc
        mix_ref[h * tq:(h + 1) * tq, :] = gate(h, 0) * _mm(pc.astype(BF16), vc)

    wwidth = WINDOW + tq
    wstart = jnp.maximum(q0 - WINDOW, 0)
    wrows = pl.ds(pl.multiple_of(wstart, tq), wwidth)
    kwin = kw_ref[0, 0, wrows, :]
    vwin = vwaug_ref[wrows, :]
    ahead = (lax.broadcasted_iota(jnp.int32, (tq, wwidth), 1)
             - lax.broadcasted_iota(jnp.int32, (tq, wwidth), 0)) + (wstart - q0)
    wmask = (ahead <= 0) & (ahead > -WINDOW)
    sw = [jnp.where(wmask, _nt(q_ref[0, h], kwin), NEG_INF) for h in heads]
    pw = [jnp.exp2(x - jnp.max(x, axis=-1, keepdims=True)).astype(BF16) for x in sw]
    ow = [_mm(x, vwin) for x in pw]
    for h in heads:
        mix_ref[h * tq:(h + 1) * tq, :] += gate(h, 2) * (ow[h][:, :HEAD_DIM] / ow[h][:, HEAD_DIM:])

    nsb = nsr
    imp_hi = imp.astype(BF16)
    imp_lo = (imp - imp_hi.astype(F32)).astype(BF16)
    brow = lax.broadcasted_iota(jnp.int32, (nsb, ncb), 0)
    ccol = lax.broadcasted_iota(jnp.int32, (nsb, ncb), 1)
    ratio = SLC_BLOCK // CMP_STRIDE
    span = CMP_BLOCK // CMP_STRIDE
    cover = ((ccol >= ratio * brow - (span - 1)) & (ccol <= ratio * brow + ratio - 1)
             & (ccol < ncb - 1)).astype(BF16)
    p_slc = _nt(cover, imp_hi) + _nt(cover, imp_lo)
    bidx = lax.broadcasted_iota(jnp.int32, (nsb, tq), 0)
    t_lane = q0 + lax.broadcasted_iota(jnp.int32, (nsb, tq), 1)
    cur = t_lane // SLC_BLOCK
    forced = (bidx == 0) | (bidx == cur) | (bidx == cur - 1)
    score = jnp.where(bidx <= cur, jnp.where(forced, FORCE_SCORE, p_slc), NEG_INF)
    score = jnp.where(bidx < n_slc, score, -3e38)
    sc_ref[...] = score
    ranks = []
    for c0 in range(0, nsb, 8):
        mine = score[c0:c0 + 8, :]
        brow8 = c0 + lax.broadcasted_iota(jnp.int32, (8, tq), 0)
        rank = jnp.zeros((8, tq), F32)
        for bp in range(n_slc):
            other = sc_ref[bp:bp + 1, :]
            ge = jnp.where(other >= mine, 1.0, 0.0)
            gt_ = jnp.where(other > mine, 1.0, 0.0)
            if bp < c0:
                rank = rank + ge
            elif bp >= c0 + 8:
                rank = rank + gt_
            else:
                rank = rank + jnp.where(brow8 > bp, ge, gt_)
        ranks.append(rank)
    rank = jnp.concatenate(ranks, axis=0)
    sel_t = jnp.where((rank < float(min(N_SELECT, n_slc))) & (bidx <= cur) & (bidx < n_slc), 0.0, -1.0)
    if nsb < LANES:
        sel_t = jnp.concatenate([sel_t, jnp.full((LANES - nsb, tq), -1.0, F32)], axis=0)
    selm1 = jnp.transpose(sel_t).astype(BF16)
    for h in heads:
        qaug_ref[h * tq:(h + 1) * tq, :HEAD_DIM] = q_ref[0, h]
        qaug_ref[h * tq:(h + 1) * tq, HEAD_DIM:] = selm1

    def init():
        m_ref[...] = jnp.full(m_ref.shape, NEG_INF, F32)
        acc_ref[...] = jnp.zeros(acc_ref.shape, F32)

    def flash(q_of, k, v, mask):
        width = k.shape[0]
        s = [_nt(q_of(h), k) for h in heads]
        p, alpha = [], []
        for h in heads:
            rows = pl.ds(h * tq, tq)
            sh = s[h] if mask is None else jnp.where(mask, s[h], NEG_INF)
            m_prev = m_ref[rows, :]
            m_next = jnp.maximum(m_prev, jnp.max(sh, axis=-1, keepdims=True))
            p.append(jnp.exp2(sh - jnp.tile(m_next, (1, width // LANES))).astype(BF16))
            alpha.append(jnp.exp2(m_prev - m_next))
            m_ref[rows, :] = m_next
        for h in heads:
            rows = pl.ds(h * tq, tq)
            acc_ref[rows, :] = jnp.tile(alpha[h], (1, 2)) * acc_ref[rows, :] + _mm(p[h], v)

    def result(h):
        rows = pl.ds(h * tq, tq)
        return acc_ref[rows, :HEAD_DIM] / acc_ref[rows, HEAD_DIM:]

    init()

    def q_slc(h):
        return qaug_ref[h * tq:(h + 1) * tq, :]

    def slc_body(j, carry):
        rows = pl.ds(pl.multiple_of(j * (SLC_STEP * tq), SLC_STEP * tq), SLC_STEP * tq)
        flash(q_slc, kaug_ref[rows, :], vaug_ref[rows, :], None)
        return carry

    lax.fori_loop(0, i // SLC_STEP, slc_body, 0)

    for left in range(SLC_STEP):
        @pl.when(i % SLC_STEP == left)
        def _(left=left):
            width = (left + 1) * tq
            rows = pl.ds(pl.multiple_of((i - left) * tq, tq), width)
            ahead = (lax.broadcasted_iota(jnp.int32, (tq, width), 1)
                     - lax.broadcasted_iota(jnp.int32, (tq, width), 0)) - left * tq
            flash(q_slc, kaug_ref[rows, :], vaug_ref[rows, :], ahead <= 0)

    for h in heads:
        merged = mix_ref[h * tq:(h + 1) * tq, :] + gate(h, 1) * result(h)
        o_ref[0, :, h * HEAD_DIM:(h + 1) * HEAD_DIM] = _rms(merged, gain_ref[0, h:h + 1, :]).astype(BF16)


def _nsa(p, kc, vc, gains, tq):
    b, _, s, _ = p.shape
    g, r = N_NSA_KV, NSA_GROUP
    ncb = kc.shape[2]
    nsr = -(-(s // SLC_BLOCK) // 8) * 8

    def kv_spec(first):
        return pl.BlockSpec((1, 1, s, LANES), lambda bb, gg, i: (bb, first + gg, 0, 0))

    return pl.pallas_call(
        functools.partial(_nsa_kernel, tq=tq),
        grid=(b, g, s // tq),
        in_specs=[
            pl.BlockSpec((1, r, tq, LANES), lambda bb, gg, i: (bb, gg, i, 0)),
            pl.BlockSpec((1, 1, ncb, LANES), lambda bb, gg, i: (bb, gg, 0, 0)),
            pl.BlockSpec((1, 1, ncb, LANES), lambda bb, gg, i: (bb, gg, 0, 0)),
            kv_spec(BLK_KS), kv_spec(BLK_VS), kv_spec(BLK_KW), kv_spec(BLK_VW),
            pl.BlockSpec((1, 1, tq, LANES), lambda bb, gg, i: (bb, BLK_GATE + gg, i, 0)),
            pl.BlockSpec((1, r, HEAD_DIM), lambda bb, gg, i: (gg, 0, 0)),
        ],
        out_specs=pl.BlockSpec((1, tq, r * HEAD_DIM), lambda bb, gg, i: (bb, i, gg)),
        out_shape=jax.ShapeDtypeStruct((b, s, N_NSA_HEADS * HEAD_DIM), BF16),
        scratch_shapes=[
            pltpu.VMEM((s, 2 * LANES), BF16),
            pltpu.VMEM((s, 2 * LANES), BF16),
            pltpu.VMEM((s, 2 * LANES), BF16),
            pltpu.VMEM((r * tq, 2 * LANES), BF16),
            pltpu.VMEM((r * tq, LANES), F32),
            pltpu.VMEM((r * tq, 2 * LANES), F32),
            pltpu.VMEM((r * tq, HEAD_DIM), F32),
            pltpu.VMEM((nsr, tq), F32),
        ],
        compiler_params=pltpu.CompilerParams(
            dimension_semantics=("parallel", "parallel", "arbitrary"), vmem_limit_bytes=VMEM_LIMIT),
        name="nsa",
    )(p, kc, vc, p, p, p, p, p, gains)


def _sb_kernel(q_ref, k_ref, v_ref, tri_ref, gain_ref, o_ref, acc_ref, c_ref, *, tk, parts):
    i = pl.program_id(2)
    acc_ref[...] = jnp.zeros(acc_ref.shape, F32)
    c_ref[...] = jnp.zeros(c_ref.shape, F32)
    tri = tri_ref[...]
    row = lax.broadcasted_iota(jnp.int32, (tk, tk), 0)
    col = lax.broadcasted_iota(jnp.int32, (tk, tk), 1)
    strict = col < row

    def tiles(k, v, work):
        rows = [pl.ds(part * tk, tk) for part, _ in work]
        zn = [_nt(q_ref[0, 0, r, :], k) for r in rows]
        lk = []
        for z, (_, masked) in zip(zn, work):
            x = jnp.minimum(z, 0.0) - jnp.log2(1.0 + jnp.exp2(-jnp.abs(z)))
            lk.append(jnp.where(strict, x, 0.0) if masked else x)
        incl = [_mm(x.astype(BF16), tri) for x in lk]
        a = []
        for n, z, r, (_, masked) in zip(incl, zn, rows, work):
            x = jnp.exp2(n + jnp.tile(c_ref[r, :], (1, tk // LANES)) - z)
            a.append((jnp.where(strict, x, 0.0) if masked else x).astype(BF16))
        for x, r in zip(a, rows):
            acc_ref[r, :] += _mm(x, v)
        for x, r in zip(lk, rows):
            c_ref[r, :] += jnp.sum(x, axis=-1, keepdims=True)

    def kv(j):
        k0 = pl.multiple_of(j * tk, tk)
        return k_ref[0, 0, pl.ds(k0, tk), :], v_ref[0, 0, pl.ds(k0, tk), :]

    for d in reversed(range(parts)):
        k, v = kv(parts * i + d)
        tiles(k, v, [(d, True)] + [(p, False) for p in range(d + 1, parts)])

    def body(jj, carry):
        k, v = kv(parts * i - 1 - jj)
        tiles(k, v, [(p, False) for p in range(parts)])
        return carry

    lax.fori_loop(0, parts * i, body, 0)
    o_ref[0] = _rms(acc_ref[...], gain_ref[0]).astype(BF16)


def _sb(p, tri, gains, tk, parts):
    b, _, s, _ = p.shape
    tq = parts * tk
    return pl.pallas_call(
        functools.partial(_sb_kernel, tk=tk, parts=parts),
        grid=(b, N_SB_HEADS, s // tq),
        in_specs=[
            pl.BlockSpec((1, 1, tq, LANES), lambda bb, h, i: (bb, BLK_QS + h, i, 0)),
            pl.BlockSpec((1, 1, s, LANES), lambda bb, h, i: (bb, BLK_KSB + h, 0, 0)),
            pl.BlockSpec((1, 1, s, LANES), lambda bb, h, i: (bb, BLK_VSB + h, 0, 0)),
            pl.BlockSpec((tk, tk), lambda bb, h, i: (0, 0)),
            pl.BlockSpec((1, 1, HEAD_DIM), lambda bb, h, i: (h, 0, 0)),
        ],
        out_specs=pl.BlockSpec((1, tq, HEAD_DIM), lambda bb, h, i: (bb, i, h)),
        out_shape=jax.ShapeDtypeStruct((b, s, N_SB_HEADS * HEAD_DIM), BF16),
        scratch_shapes=[
            pltpu.VMEM((tq, HEAD_DIM), F32),
            pltpu.VMEM((tq, LANES), F32),
        ],
        compiler_params=pltpu.CompilerParams(
            dimension_semantics=("parallel", "parallel", "arbitrary"), vmem_limit_bytes=VMEM_LIMIT),
        name="sb",
    )(p, p, p, tri, gains)


def _out_kernel(x_ref, ya_ref, yb_ref, wa_ref, wb_ref, o_ref):
    o_ref[...] = x_ref[...] + _mm(ya_ref[...], wa_ref[...]) + _mm(yb_ref[...], wb_ref[...])


def _out_proj(x2d, ya, yb, wa, wb, tm):
    t, d = x2d.shape
    ka, kb = ya.shape[1], yb.shape[1]
    return pl.pallas_call(
        _out_kernel,
        grid=(t // tm,),
        in_specs=[
            pl.BlockSpec((tm, d), lambda i: (i, 0)),
            pl.BlockSpec((tm, ka), lambda i: (i, 0)),
            pl.BlockSpec((tm, kb), lambda i: (i, 0)),
            pl.BlockSpec((ka, d), lambda i: (0, 0)),
            pl.BlockSpec((kb, d), lambda i: (0, 0)),
        ],
        out_specs=pl.BlockSpec((tm, d), lambda i: (i, 0)),
        out_shape=jax.ShapeDtypeStruct((t, d), F32),
        compiler_params=pltpu.CompilerParams(
            dimension_semantics=("parallel",), vmem_limit_bytes=VMEM_LIMIT),
        name="out_proj",
    )(x2d, ya, yb, wa, wb)


def _tile(n, pref):
    t = min(n, pref)
    while n % t:
        t //= 2
    return t


class _Plan(NamedTuple):
    tm_ffn: int
    tf: int
    tm_proj: int
    tm_out: int
    tq_nsa: int
    tk_sb: int
    parts_sb: int


def _plan(b, s, ff):
    plan = _Plan(tm_ffn=_tile(b * s, 1024), tf=_tile(ff, 512), tm_proj=_tile(s, 1024), tm_out=_tile(b * s, 512),
                 tq_nsa=MXU_TILE, tk_sb=MXU_TILE, parts_sb=max(1, min(16, s // MXU_TILE)))
    assert s % (plan.tk_sb * plan.parts_sb) == 0 and s % plan.tq_nsa == 0
    assert s // SLC_BLOCK <= LANES and WINDOW % plan.tq_nsa == 0 and s >= WINDOW + plan.tq_nsa
    return plan


def _pack_w_in(w_in):
    d = w_in.shape[0]
    hd = HEAD_DIM
    kv = N_NSA_KV * hd
    nq = N_NSA_HEADS * hd
    o_gate = nq + 6 * kv
    o_sb = o_gate + N_NSA_HEADS * 3
    gate = w_in[:, o_gate:o_sb].reshape(d, N_NSA_KV, NSA_GROUP * 3)
    gate = jnp.pad(gate, ((0, 0), (0, 0), (0, hd - NSA_GROUP * 3))).reshape(d, N_NSA_KV * hd)
    pad = jnp.zeros((d, (BLK_QS - BLK_GATE - N_NSA_KV) * hd), w_in.dtype)
    return jnp.concatenate([w_in[:, :o_gate], gate, pad, w_in[:, o_sb:]], axis=1).astype(BF16)


def kernel(x, positions, ffn1_norm, ffn1_w_gate, ffn1_w_up, ffn1_w_down, mix_norm, w_in,
           nsa_q_norm, nsa_k_cmp_norm, nsa_k_slc_norm, nsa_k_win_norm,
           cmp_k_pos, cmp_k_w1, cmp_k_w2, cmp_v_pos, cmp_v_w1, cmp_v_w2,
           nsa_out_norm, sb_out_norm, w_out, ffn2_norm, ffn2_w_gate, ffn2_w_up, ffn2_w_down):
    b, s, d = x.shape
    depth = w_in.shape[0]
    hd = HEAD_DIM
    hid = cmp_k_w2.shape[1]
    t = b * s
    plan = _plan(b, s, ffn1_w_gate.shape[2])

    inv_freq = ROPE_THETA ** (-jnp.arange(0, hd, 2, dtype=F32) / hd)
    ang = positions.astype(F32)[..., None] * inv_freq
    cosf = jnp.concatenate([jnp.cos(ang), jnp.cos(ang)], axis=-1)
    sinf = jnp.concatenate([-jnp.sin(ang), jnp.sin(ang)], axis=-1)

    idx = jnp.arange(plan.tk_sb, dtype=jnp.int32)
    tri = (idx[:, None] >= idx[None, :]).astype(BF16)

    def w1cat(w1):
        w = w1.reshape(CMP_BLOCK, hd, hid)
        return jnp.concatenate([w[:CMP_STRIDE], w[CMP_STRIDE:]], axis=-1).astype(BF16)

    for l in range(depth):
        x2 = _ffn(x.reshape(t, d), ffn1_norm[l], ffn1_w_gate[l].astype(BF16), ffn1_w_up[l].astype(BF16),
                  ffn1_w_down[l].astype(BF16), plan.tm_ffn, plan.tf)

        head_gains = jnp.zeros((8, hd), F32)
        head_gains = head_gains.at[0].set(nsa_q_norm[l] * (hd ** -0.5 * LOG2E)).at[1].set(nsa_k_slc_norm[l]).at[2].set(nsa_k_win_norm[l])
        p, csrc = _in_proj(x2.reshape(b, s, d), mix_norm[l], _pack_w_in(w_in[l]), cosf, sinf, head_gains,
                           plan.tm_proj)

        kc = _compress(csrc, 0, cmp_k_pos[l], w1cat(cmp_k_w1[l]), cmp_k_w2[l].astype(BF16),
                       nsa_k_cmp_norm[l], True, F32)
        vc = _compress(csrc, N_NSA_KV, cmp_v_pos[l], w1cat(cmp_v_w1[l]), cmp_v_w2[l].astype(BF16),
                       jnp.ones((hd,), F32), False, BF16)

        y_nsa = _nsa(p, kc, vc, nsa_out_norm[l].reshape(N_NSA_KV, NSA_GROUP, hd), plan.tq_nsa)
        y_sb = _sb(p, tri, sb_out_norm[l].reshape(N_SB_HEADS, 1, hd), plan.tk_sb, plan.parts_sb)

        nmix = N_NSA_HEADS * hd
        x3 = _out_proj(x2, y_nsa.reshape(t, nmix), y_sb.reshape(t, N_SB_HEADS * hd),
                       w_out[l, :nmix].astype(BF16), w_out[l, nmix:].astype(BF16), plan.tm_out)

        x = _ffn(x3, ffn2_norm[l], ffn2_w_gate[l].astype(BF16), ffn2_w_up[l].astype(BF16),
                 ffn2_w_down[l].astype(BF16), plan.tm_ffn, plan.tf).reshape(b, s, d)
    return x
```

```python
import functools
from typing import NamedTuple

import jax
import jax.numpy as jnp
from jax import lax
from jax.experimental import pallas as pl
from jax.experimental.pallas import tpu as pltpu

HEAD_DIM = 128
N_NSA_HEADS = 8
N_NSA_KV = 2
NSA_GROUP = N_NSA_HEADS // N_NSA_KV
N_SB_HEADS = 8
CMP_BLOCK = 32
CMP_STRIDE = 16
SLC_BLOCK = 64
N_SELECT = 16
WINDOW = 512
ROPE_THETA = 10000.0
EPS = 1e-6
NEG_INF = -1e30
FORCE_SCORE = 1e9
LOG2E = 1.4426950408889634
MASK_SCORE = 1e30

LANES = 128
MXU_TILE = 256
VMEM_LIMIT = 56 * 1024 * 1024

BLK_Q = 0
BLK_KC, BLK_VC = 8, 10
BLK_KS, BLK_VS = 12, 14
BLK_KW, BLK_VW = 16, 18
BLK_GATE = 20
BLK_QS, BLK_KSB, BLK_VSB = 24, 32, 40
N_BLK = 48
PROJ_NB = 8
SLC_STEP = 4

BF16 = jnp.bfloat16
F32 = jnp.float32


def _nt(a, b):
    return lax.dot_general(a, b, (((1,), (1,)), ((), ())), preferred_element_type=F32)


def _mm(a, b):
    return jnp.dot(a, b, preferred_element_type=F32)


def _rms(v, g):
    return v * lax.rsqrt(jnp.mean(v * v, axis=-1, keepdims=True) + EPS) * g


def _ffn_kernel(x_ref, g_ref, wg_ref, wu_ref, wd_ref, o_ref, xn_ref):
    j = pl.program_id(1)

    @pl.when(j == 0)
    def _():
        xn_ref[...] = _rms(x_ref[...], g_ref[...]).astype(BF16)
        o_ref[...] = jnp.zeros_like(o_ref)

    xn = xn_ref[...]
    h = _mm(xn, wg_ref[...])
    u = _mm(xn, wu_ref[...])
    a = (h * (1.0 / (1.0 + jnp.exp(-h)))) * u
    o_ref[...] += _mm(a.astype(BF16), wd_ref[...])

    @pl.when(j == pl.num_programs(1) - 1)
    def _():
        o_ref[...] = x_ref[...] + 0.5 * o_ref[...]


def _ffn(x2d, gain, wg, wu, wd, tm, tf):
    t, d = x2d.shape
    ff = wg.shape[1]
    return pl.pallas_call(
        _ffn_kernel,
        grid=(t // tm, ff // tf),
        in_specs=[
            pl.BlockSpec((tm, d), lambda i, j: (i, 0)),
            pl.BlockSpec((1, d), lambda i, j: (0, 0)),
            pl.BlockSpec((d, tf), lambda i, j: (0, j)),
            pl.BlockSpec((d, tf), lambda i, j: (0, j)),
            pl.BlockSpec((tf, d), lambda i, j: (j, 0)),
        ],
        out_specs=pl.BlockSpec((tm, d), lambda i, j: (i, 0)),
        out_shape=jax.ShapeDtypeStruct((t, d), F32),
        scratch_shapes=[pltpu.VMEM((tm, d), BF16)],
        compiler_params=pltpu.CompilerParams(
            dimension_semantics=("parallel", "arbitrary"), vmem_limit_bytes=VMEM_LIMIT),
        name="ffn",
    )(x2d, gain.reshape(1, d), wg, wu, wd)


def _proj_kernel(x_ref, g_ref, w_ref, cos_ref, sin_ref, hg_ref, p_ref, c_ref, xn_ref):
    j = pl.program_id(2)
    scale = HEAD_DIM ** -0.5

    @pl.when(j == 0)
    def _():
        xn_ref[...] = _rms(x_ref[0], g_ref[...]).astype(BF16)

    acc = _mm(xn_ref[...], w_ref[...])

    def blk(c):
        return acc[:, c * LANES:(c + 1) * LANES]

    def rope(v):
        return v * cos_ref[0] + pltpu.roll(v, HEAD_DIM // 2, 1) * sin_ref[0]

    def put(c, v):
        p_ref[0, c] = v.astype(BF16)

    def finish(b, c):
        v = blk(c)
        if b < BLK_KC:
            put(c, rope(_rms(v, hg_ref[0:1, :])))
        elif b < BLK_KS:
            v = rope(v) if b < BLK_VC else v
            put(c, v)
            c_ref[0, b - BLK_KC] = v
        elif BLK_KS <= b < BLK_VS:
            put(c, rope(_rms(v, hg_ref[1:2, :])))
        elif BLK_KW <= b < BLK_VW:
            put(c, rope(_rms(v, hg_ref[2:3, :])))
        elif BLK_QS <= b < BLK_KSB:
            put(c, v * (-scale * LOG2E))
        else:
            put(c, v)

    for step in range(N_BLK // PROJ_NB):
        @pl.when(j == step)
        def _(step=step):
            for c in range(PROJ_NB):
                finish(step * PROJ_NB + c, c)


def _in_proj(x, gain, w, cosf, sinf, head_gains, tm):
    b, s, d = x.shape
    tn = PROJ_NB * LANES
    return pl.pallas_call(
        _proj_kernel,
        grid=(b, s // tm, N_BLK // PROJ_NB),
        in_specs=[
            pl.BlockSpec((1, tm, d), lambda bb, i, j: (bb, i, 0)),
            pl.BlockSpec((1, d), lambda bb, i, j: (0, 0)),
            pl.BlockSpec((d, tn), lambda bb, i, j: (0, j)),
            pl.BlockSpec((1, tm, LANES), lambda bb, i, j: (bb, i, 0)),
            pl.BlockSpec((1, tm, LANES), lambda bb, i, j: (bb, i, 0)),
            pl.BlockSpec((8, LANES), lambda bb, i, j: (0, 0)),
        ],
        out_specs=[
            pl.BlockSpec((1, PROJ_NB, tm, LANES), lambda bb, i, j: (bb, j, i, 0)),
            pl.BlockSpec((1, 4, tm, LANES), lambda bb, i, j: (bb, 0, i, 0)),
        ],
        out_shape=[
            jax.ShapeDtypeStruct((b, N_BLK, s, LANES), BF16),
            jax.ShapeDtypeStruct((b, 4, s, LANES), F32),
        ],
        scratch_shapes=[pltpu.VMEM((tm, d), BF16)],
        compiler_params=pltpu.CompilerParams(
            dimension_semantics=("parallel", "parallel", "arbitrary"), vmem_limit_bytes=VMEM_LIMIT),
        name="in_proj",
    )(x, gain.reshape(1, d), w, cosf, sinf, head_gains)


def _compress_kernel(x_ref, pos_ref, w1_ref, w2_ref, g_ref, o_ref, *, normalize):
    nb = o_ref.shape[2]
    hid = w2_ref.shape[0]
    acc = jnp.zeros((nb, 2 * hid), F32)
    bias = jnp.zeros((8, hid), F32)
    for l in range(CMP_STRIDE):
        rows = x_ref[0, 0, pl.ds(l, nb, stride=CMP_STRIDE), :]
        acc = acc + _mm(rows.astype(BF16), w1_ref[l])
        p_lo = jnp.broadcast_to(pos_ref[l:l + 1, :], (8, HEAD_DIM)).astype(BF16)
        p_hi = jnp.broadcast_to(pos_ref[l + CMP_STRIDE:l + CMP_STRIDE + 1, :], (8, HEAD_DIM)).astype(BF16)
        bias = bias + _mm(p_lo, w1_ref[l, :, :hid]) + _mm(p_hi, w1_ref[l, :, hid:])
    hidden = acc[:, :hid] + pltpu.roll(acc[:, hid:], nb - 1, 0) + bias[0:1, :]
    act = 0.5 * hidden * (1.0 + jnp.tanh(0.7978845608028654 * (hidden + 0.044715 * hidden * hidden * hidden)))
    out = _mm(act.astype(BF16), w2_ref[...])
    if normalize:
        out = _rms(out, g_ref[...])
    row = lax.broadcasted_iota(jnp.int32, out.shape, 0)
    o_ref[0, 0] = jnp.where(row < nb - 1, out, 0.0).astype(o_ref.dtype)


def _compress(src, first_blk, pos, w1cat, w2, gain, normalize, out_dtype):
    b, _, s, _ = src.shape
    nb = s // CMP_STRIDE
    hid = w2.shape[0]
    return pl.pallas_call(
        functools.partial(_compress_kernel, normalize=normalize),
        grid=(b, N_NSA_KV),
        in_specs=[
            pl.BlockSpec((1, 1, s, LANES), lambda bb, g: (bb, first_blk + g, 0, 0)),
            pl.BlockSpec((CMP_BLOCK, HEAD_DIM), lambda bb, g: (0, 0)),
            pl.BlockSpec((CMP_STRIDE, HEAD_DIM, 2 * hid), lambda bb, g: (0, 0, 0)),
            pl.BlockSpec((hid, HEAD_DIM), lambda bb, g: (0, 0)),
            pl.BlockSpec((1, HEAD_DIM), lambda bb, g: (0, 0)),
        ],
        out_specs=pl.BlockSpec((1, 1, nb, HEAD_DIM), lambda bb, g: (bb, g, 0, 0)),
        out_shape=jax.ShapeDtypeStruct((b, N_NSA_KV, nb, HEAD_DIM), out_dtype),
        compiler_params=pltpu.CompilerParams(
            dimension_semantics=("parallel", "parallel"), vmem_limit_bytes=VMEM_LIMIT),
        name="compress_k" if normalize else "compress_v",
    )(src, pos, w1cat, w2, gain.reshape(1, HEAD_DIM))


def _nsa_kernel(q_ref, kc_ref, vc_ref, ks_ref, vs_ref, kw_ref, vw_ref, gate_ref, gain_ref,
                o_ref, kaug_ref, vaug_ref, vwaug_ref, qaug_ref, m_ref, acc_ref, mix_ref, sc_ref, rk_ref, *, tq):
    i = pl.program_id(2)
    r = NSA_GROUP
    q0 = i * tq
    s_len = ks_ref.shape[2]
    ncb = kc_ref.shape[2]
    n_slc = s_len // SLC_BLOCK
    nsr = -(-n_slc // 8) * 8
    heads = range(r)

    @pl.when(i == 0)
    def _():
        lane = lax.broadcasted_iota(jnp.int32, (s_len, LANES), 1)
        kblk = lax.broadcasted_iota(jnp.int32, (s_len, LANES), 0) // SLC_BLOCK
        kaug_ref[:, :HEAD_DIM] = ks_ref[0, 0]
        kaug_ref[:, HEAD_DIM:] = jnp.where(lane == kblk, MASK_SCORE, 0.0).astype(BF16)
        ones = jnp.ones((s_len, LANES), BF16)
        vaug_ref[:, :HEAD_DIM] = vs_ref[0, 0]
        vaug_ref[:, HEAD_DIM:] = ones
        vwaug_ref[:, :HEAD_DIM] = vw_ref[0, 0]
        vwaug_ref[:, HEAD_DIM:] = ones

    gt = 1.0 / (1.0 + jnp.exp(-gate_ref[0, 0].astype(F32)))

    def gate(h, c):
        return gt[:, 3 * h + c:3 * h + c + 1]

    kc = kc_ref[0, 0].astype(BF16)
    vc = vc_ref[0, 0]
    cend = lax.broadcasted_iota(jnp.int32, (tq, ncb), 1) * CMP_STRIDE + (CMP_BLOCK - 1)
    cvalid = cend <= q0 + lax.broadcasted_iota(jnp.int32, (tq, ncb), 0)
    imp = jnp.zeros((tq, ncb), F32)
    for h in heads:
        s = jnp.where(cvalid, _nt(q_ref[0, h], kc), NEG_INF)
        e = jnp.where(cvalid, jnp.exp2(s - jnp.max(s, axis=-1, keepdims=True)), 0.0)
        den = jnp.sum(e, axis=-1, keepdims=True)
        pc = e * (1.0 / jnp.where(den > 0.0, den, 1.0))
        imp = imp + pc
        mix_ref[h * tq:(h + 1) * tq, :] = gate(h, 0) * _mm(pc.astype(BF16), vc)

    wwidth = WINDOW + tq
    wstart = jnp.maximum(q0 - WINDOW, 0)
    wrows = pl.ds(pl.multiple_of(wstart, tq), wwidth)
    kwin = kw_ref[0, 0, wrows, :]
    vwin = vwaug_ref[wrows, :]
    ahead = (lax.broadcasted_iota(jnp.int32, (tq, wwidth), 1)
             - lax.broadcasted_iota(jnp.int32, (tq, wwidth), 0)) + (wstart - q0)
    wmask = (ahead <= 0) & (ahead > -WINDOW)
    sw = [jnp.where(wmask, _nt(q_ref[0, h], kwin), NEG_INF) for h in heads]
    pw = [jnp.exp2(x - jnp.max(x, axis=-1, keepdims=True)).astype(BF16) for x in sw]
    ow = [_mm(x, vwin) for x in pw]
    for h in heads:
        mix_ref[h * tq:(h + 1) * tq, :] += gate(h, 2) * (ow[h][:, :HEAD_DIM] / ow[h][:, HEAD_DIM:])

    nsb = nsr
    imp_hi = imp.astype(BF16)
    imp_lo = (imp - imp_hi.astype(F32)).astype(BF16)
    brow = lax.broadcasted_iota(jnp.int32, (nsb, ncb), 0)
    ccol = lax.broadcasted_iota(jnp.int32, (nsb, ncb), 1)
    ratio = SLC_BLOCK // CMP_STRIDE
    span = CMP_BLOCK // CMP_STRIDE
    cover = ((ccol >= ratio * brow - (span - 1)) & (ccol <= ratio * brow + ratio - 1)
             & (ccol < ncb - 1)).astype(BF16)
    p_slc = _nt(cover, imp_hi) + _nt(cover, imp_lo)
    bidx = lax.broadcasted_iota(jnp.int32, (nsb, tq), 0)
    t_lane = q0 + lax.broadcasted_iota(jnp.int32, (nsb, tq), 1)
    cur = t_lane // SLC_BLOCK
    forced = (bidx == 0) | (bidx == cur) | (bidx == cur - 1)
    score = jnp.where(bidx <= cur, jnp.where(forced, FORCE_SCORE, p_slc), NEG_INF)
    score = jnp.where(bidx < n_slc, score, -3e38)
    sc_ref[...] = score
    rk_ref[...] = jnp.full((nsb, tq), float(N_SELECT), F32)
    n_causal = (q0 + tq - 1) // SLC_BLOCK + 1
    for c0 in range(0, nsb, 8):
        @pl.when(c0 < n_causal)
        def _(c0=c0):
            mine = sc_ref[c0:c0 + 8, :]
            brow8 = c0 + lax.broadcasted_iota(jnp.int32, (8, tq), 0)
            rank8 = jnp.zeros((8, tq), F32)
            for bp in range(n_slc):
                other = sc_ref[bp:bp + 1, :]
                ge = jnp.where(other >= mine, 1.0, 0.0)
                gt_ = jnp.where(other > mine, 1.0, 0.0)
                if bp < c0:
                    rank8 = rank8 + ge
                elif bp >= c0 + 8:
                    rank8 = rank8 + gt_
                else:
                    rank8 = rank8 + jnp.where(brow8 > bp, ge, gt_)
            rk_ref[c0:c0 + 8, :] = rank8
    rank = rk_ref[...]
    sel_t = jnp.where((rank < float(min(N_SELECT, n_slc))) & (bidx <= cur) & (bidx < n_slc), 0.0, -1.0)
    if nsb < LANES:
        sel_t = jnp.concatenate([sel_t, jnp.full((LANES - nsb, tq), -1.0, F32)], axis=0)
    selm1 = jnp.transpose(sel_t).astype(BF16)
    for h in heads:
        qaug_ref[h * tq:(h + 1) * tq, :HEAD_DIM] = q_ref[0, h]
        qaug_ref[h * tq:(h + 1) * tq, HEAD_DIM:] = selm1

    def init():
        m_ref[...] = jnp.full(m_ref.shape, NEG_INF, F32)
        acc_ref[...] = jnp.zeros(acc_ref.shape, F32)

    def flash(q_of, k, v, mask):
        width = k.shape[0]
        s = [_nt(q_of(h), k) for h in heads]
        p, alpha = [], []
        for h in heads:
            rows = pl.ds(h * tq, tq)
            sh = s[h] if mask is None else jnp.where(mask, s[h], NEG_INF)
            m_prev = m_ref[rows, :]
            m_next = jnp.maximum(m_prev, jnp.max(sh, axis=-1, keepdims=True))
            p.append(jnp.exp2(sh - jnp.tile(m_next, (1, width // LANES))).astype(BF16))
            alpha.append(jnp.exp2(m_prev - m_next))
            m_ref[rows, :] = m_next
        for h in heads:
            rows = pl.ds(h * tq, tq)
            acc_ref[rows, :] = jnp.tile(alpha[h], (1, 2)) * acc_ref[rows, :] + _mm(p[h], v)

    def result(h):
        rows = pl.ds(h * tq, tq)
        return acc_ref[rows, :HEAD_DIM] / acc_ref[rows, HEAD_DIM:]

    init()

    def q_slc(h):
        return qaug_ref[h * tq:(h + 1) * tq, :]

    def slc_body(j, carry):
        rows = pl.ds(pl.multiple_of(j * (SLC_STEP * tq), SLC_STEP * tq), SLC_STEP * tq)
        flash(q_slc, kaug_ref[rows, :], vaug_ref[rows, :], None)
        return carry

    lax.fori_loop(0, i // SLC_STEP, slc_body, 0)

    for left in range(SLC_STEP):
        @pl.when(i % SLC_STEP == left)
        def _(left=left):
            width = (left + 1) * tq
            rows = pl.ds(pl.multiple_of((i - left) * tq, tq), width)
            ahead = (lax.broadcasted_iota(jnp.int32, (tq, width), 1)
                     - lax.broadcasted_iota(jnp.int32, (tq, width), 0)) - left * tq
            flash(q_slc, kaug_ref[rows, :], vaug_ref[rows, :], ahead <= 0)

    for h in heads:
        merged = mix_ref[h * tq:(h + 1) * tq, :] + gate(h, 1) * result(h)
        o_ref[0, :, h * HEAD_DIM:(h + 1) * HEAD_DIM] = _rms(merged, gain_ref[0, h:h + 1, :]).astype(BF16)


def _nsa(p, kc, vc, gains, tq):
    b, _, s, _ = p.shape
    g, r = N_NSA_KV, NSA_GROUP
    ncb = kc.shape[2]
    nsr = -(-(s // SLC_BLOCK) // 8) * 8

    def kv_spec(first):
        return pl.BlockSpec((1, 1, s, LANES), lambda bb, gg, i: (bb, first + gg, 0, 0))

    return pl.pallas_call(
        functools.partial(_nsa_kernel, tq=tq),
        grid=(b, g, s // tq),
        in_specs=[
            pl.BlockSpec((1, r, tq, LANES), lambda bb, gg, i: (bb, gg, i, 0)),
            pl.BlockSpec((1, 1, ncb, LANES), lambda bb, gg, i: (bb, gg, 0, 0)),
            pl.BlockSpec((1, 1, ncb, LANES), lambda bb, gg, i: (bb, gg, 0, 0)),
            kv_spec(BLK_KS), kv_spec(BLK_VS), kv_spec(BLK_KW), kv_spec(BLK_VW),
            pl.BlockSpec((1, 1, tq, LANES), lambda bb, gg, i: (bb, BLK_GATE + gg, i, 0)),
            pl.BlockSpec((1, r, HEAD_DIM), lambda bb, gg, i: (gg, 0, 0)),
        ],
        out_specs=pl.BlockSpec((1, tq, r * HEAD_DIM), lambda bb, gg, i: (bb, i, gg)),
        out_shape=jax.ShapeDtypeStruct((b, s, N_NSA_HEADS * HEAD_DIM), BF16),
        scratch_shapes=[
            pltpu.VMEM((s, 2 * LANES), BF16),
            pltpu.VMEM((s, 2 * LANES), BF16),
            pltpu.VMEM((s, 2 * LANES), BF16),
            pltpu.VMEM((r * tq, 2 * LANES), BF16),
            pltpu.VMEM((r * tq, LANES), F32),
            pltpu.VMEM((r * tq, 2 * LANES), F32),
            pltpu.VMEM((r * tq, HEAD_DIM), F32),
            pltpu.VMEM((nsr, tq), F32),
            pltpu.VMEM((nsr, tq), F32),
        ],
        compiler_params=pltpu.CompilerParams(
            dimension_semantics=("parallel", "parallel", "arbitrary"), vmem_limit_bytes=VMEM_LIMIT),
        name="nsa",
    )(p, kc, vc, p, p, p, p, p, gains)


def _sb_kernel(q_ref, k_ref, v_ref, tri_ref, gain_ref, o_ref, acc_ref, c_ref, *, tk, parts):
    i = pl.program_id(2)
    acc_ref[...] = jnp.zeros(acc_ref.shape, F32)
    c_ref[...] = jnp.zeros(c_ref.shape, F32)
    tri = tri_ref[...]
    row = lax.broadcasted_iota(jnp.int32, (tk, tk), 0)
    col = lax.broadcasted_iota(jnp.int32, (tk, tk), 1)
    strict = col < row

    def tiles(k, v, work):
        rows = [pl.ds(part * tk, tk) for part, _ in work]
        zn = [_nt(q_ref[0, 0, r, :], k) for r in rows]
        lk = []
        for z, (_, masked) in zip(zn, work):
            x = jnp.minimum(z, 0.0) - jnp.log2(1.0 + jnp.exp2(-jnp.abs(z)))
            lk.append(jnp.where(strict, x, 0.0) if masked else x)
        incl = [_mm(x.astype(BF16), tri) for x in lk]
        a = []
        for n, z, r, (_, masked) in zip(incl, zn, rows, work):
            x = jnp.exp2(n + jnp.tile(c_ref[r, :], (1, tk // LANES)) - z)
            a.append((jnp.where(strict, x, 0.0) if masked else x).astype(BF16))
        for x, r in zip(a, rows):
            acc_ref[r, :] += _mm(x, v)
        for x, r in zip(lk, rows):
            c_ref[r, :] += jnp.sum(x, axis=-1, keepdims=True)

    def kv(j):
        k0 = pl.multiple_of(j * tk, tk)
        return k_ref[0, 0, pl.ds(k0, tk), :], v_ref[0, 0, pl.ds(k0, tk), :]

    for d in reversed(range(parts)):
        k, v = kv(parts * i + d)
        tiles(k, v, [(d, True)] + [(p, False) for p in range(d + 1, parts)])

    def body(jj, carry):
        k, v = kv(parts * i - 1 - jj)
        tiles(k, v, [(p, False) for p in range(parts)])
        return carry

    lax.fori_loop(0, parts * i, body, 0)
    o_ref[0] = _rms(acc_ref[...], gain_ref[0]).astype(BF16)


def _sb(p, tri, gains, tk, parts):
    b, _, s, _ = p.shape
    tq = parts * tk
    return pl.pallas_call(
        functools.partial(_sb_kernel, tk=tk, parts=parts),
        grid=(b, N_SB_HEADS, s // tq),
        in_specs=[
            pl.BlockSpec((1, 1, tq, LANES), lambda bb, h, i: (bb, BLK_QS + h, i, 0)),
            pl.BlockSpec((1, 1, s, LANES), lambda bb, h, i: (bb, BLK_KSB + h, 0, 0)),
            pl.BlockSpec((1, 1, s, LANES), lambda bb, h, i: (bb, BLK_VSB + h, 0, 0)),
            pl.BlockSpec((tk, tk), lambda bb, h, i: (0, 0)),
            pl.BlockSpec((1, 1, HEAD_DIM), lambda bb, h, i: (h, 0, 0)),
        ],
        out_specs=pl.BlockSpec((1, tq, HEAD_DIM), lambda bb, h, i: (bb, i, h)),
        out_shape=jax.ShapeDtypeStruct((b, s, N_SB_HEADS * HEAD_DIM), BF16),
        scratch_shapes=[
            pltpu.VMEM((tq, HEAD_DIM), F32),
            pltpu.VMEM((tq, LANES), F32),
        ],
        compiler_params=pltpu.CompilerParams(
            dimension_semantics=("parallel", "parallel", "arbitrary"), vmem_limit_bytes=VMEM_LIMIT),
        name="sb",
    )(p, p, p, tri, gains)


def _out_kernel(x_ref, ya_ref, yb_ref, wa_ref, wb_ref, o_ref):
    o_ref[...] = x_ref[...] + _mm(ya_ref[...], wa_ref[...]) + _mm(yb_ref[...], wb_ref[...])


def _out_proj(x2d, ya, yb, wa, wb, tm):
    t, d = x2d.shape
    ka, kb = ya.shape[1], yb.shape[1]
    return pl.pallas_call(
        _out_kernel,
        grid=(t // tm,),
        in_specs=[
            pl.BlockSpec((tm, d), lambda i: (i, 0)),
            pl.BlockSpec((tm, ka), lambda i: (i, 0)),
            pl.BlockSpec((tm, kb), lambda i: (i, 0)),
            pl.BlockSpec((ka, d), lambda i: (0, 0)),
            pl.BlockSpec((kb, d), lambda i: (0, 0)),
        ],
        out_specs=pl.BlockSpec((tm, d), lambda i: (i, 0)),
        out_shape=jax.ShapeDtypeStruct((t, d), F32),
        compiler_params=pltpu.CompilerParams(
            dimension_semantics=("parallel",), vmem_limit_bytes=VMEM_LIMIT),
        name="out_proj",
    )(x2d, ya, yb, wa, wb)


def _tile(n, pref):
    t = min(n, pref)
    while n % t:
        t //= 2
    return t


class _Plan(NamedTuple):
    tm_ffn: int
    tf: int
    tm_proj: int
    tm_out: int
    tq_nsa: int
    tk_sb: int
    parts_sb: int


def _plan(b, s, ff):
    plan = _Plan(tm_ffn=_tile(b * s, 1024), tf=_tile(ff, 512), tm_proj=_tile(s, 1024), tm_out=_tile(b * s, 512),
                 tq_nsa=MXU_TILE, tk_sb=MXU_TILE, parts_sb=max(1, min(16, s // MXU_TILE)))
    assert s % (plan.tk_sb * plan.parts_sb) == 0 and s % plan.tq_nsa == 0
    assert s // SLC_BLOCK <= LANES and WINDOW % plan.tq_nsa == 0 and s >= WINDOW + plan.tq_nsa
    return plan


def _pack_w_in(w_in):
    d = w_in.shape[0]
    hd = HEAD_DIM
    kv = N_NSA_KV * hd
    nq = N_NSA_HEADS * hd
    o_gate = nq + 6 * kv
    o_sb = o_gate + N_NSA_HEADS * 3
    gate = w_in[:, o_gate:o_sb].reshape(d, N_NSA_KV, NSA_GROUP * 3)
    gate = jnp.pad(gate, ((0, 0), (0, 0), (0, hd - NSA_GROUP * 3))).reshape(d, N_NSA_KV * hd)
    pad = jnp.zeros((d, (BLK_QS - BLK_GATE - N_NSA_KV) * hd), w_in.dtype)
    return jnp.concatenate([w_in[:, :o_gate], gate, pad, w_in[:, o_sb:]], axis=1).astype(BF16)


def kernel(x, positions, ffn1_norm, ffn1_w_gate, ffn1_w_up, ffn1_w_down, mix_norm, w_in,
           nsa_q_norm, nsa_k_cmp_norm, nsa_k_slc_norm, nsa_k_win_norm,
           cmp_k_pos, cmp_k_w1, cmp_k_w2, cmp_v_pos, cmp_v_w1, cmp_v_w2,
           nsa_out_norm, sb_out_norm, w_out, ffn2_norm, ffn2_w_gate, ffn2_w_up, ffn2_w_down):
    b, s, d = x.shape
    depth = w_in.shape[0]
    hd = HEAD_DIM
    hid = cmp_k_w2.shape[1]
    t = b * s
    plan = _plan(b, s, ffn1_w_gate.shape[2])

    inv_freq = ROPE_THETA ** (-jnp.arange(0, hd, 2, dtype=F32) / hd)
    ang = positions.astype(F32)[..., None] * inv_freq
    cosf = jnp.concatenate([jnp.cos(ang), jnp.cos(ang)], axis=-1)
    sinf = jnp.concatenate([-jnp.sin(ang), jnp.sin(ang)], axis=-1)

    idx = jnp.arange(plan.tk_sb, dtype=jnp.int32)
    tri = (idx[:, None] >= idx[None, :]).astype(BF16)

    def w1cat(w1):
        w = w1.reshape(CMP_BLOCK, hd, hid)
        return jnp.concatenate([w[:CMP_STRIDE], w[CMP_STRIDE:]], axis=-1).astype(BF16)

    for l in range(depth):
        x2 = _ffn(x.reshape(t, d), ffn1_norm[l], ffn1_w_gate[l].astype(BF16), ffn1_w_up[l].astype(BF16),
                  ffn1_w_down[l].astype(BF16), plan.tm_ffn, plan.tf)

        head_gains = jnp.zeros((8, hd), F32)
        head_gains = head_gains.at[0].set(nsa_q_norm[l] * (hd ** -0.5 * LOG2E)).at[1].set(nsa_k_slc_norm[l]).at[2].set(nsa_k_win_norm[l])
        p, csrc = _in_proj(x2.reshape(b, s, d), mix_norm[l], _pack_w_in(w_in[l]), cosf, sinf, head_gains,
                           plan.tm_proj)

        kc = _compress(csrc, 0, cmp_k_pos[l], w1cat(cmp_k_w1[l]), cmp_k_w2[l].astype(BF16),
                       nsa_k_cmp_norm[l], True, F32)
        vc = _compress(csrc, N_NSA_KV, cmp_v_pos[l], w1cat(cmp_v_w1[l]), cmp_v_w2[l].astype(BF16),
                       jnp.ones((hd,), F32), False, BF16)

        y_nsa = _nsa(p, kc, vc, nsa_out_norm[l].reshape(N_NSA_KV, NSA_GROUP, hd), plan.tq_nsa)
        y_sb = _sb(p, tri, sb_out_norm[l].reshape(N_SB_HEADS, 1, hd), plan.tk_sb, plan.parts_sb)

        nmix = N_NSA_HEADS * hd
        x3 = _out_proj(x2, y_nsa.reshape(t, nmix), y_sb.reshape(t, N_SB_HEADS * hd),
                       w_out[l, :nmix].astype(BF16), w_out[l, nmix:].astype(BF16), plan.tm_out)

        x = _ffn(x3, ffn2_norm[l], ffn2_w_gate[l].astype(BF16), ffn2_w_up[l].astype(BF16),
                 ffn2_w_down[l].astype(BF16), plan.tm_ffn, plan.tf).reshape(b, s, d)
    return x
```

```python
import functools
from typing import NamedTuple

import jax
import jax.numpy as jnp
from jax import lax
from jax.experimental import pallas as pl
from jax.experimental.pallas import tpu as pltpu

HEAD_DIM = 128
N_NSA_HEADS = 8
N_NSA_KV = 2
NSA_GROUP = N_NSA_HEADS // N_NSA_KV
N_SB_HEADS = 8
CMP_BLOCK = 32
CMP_STRIDE = 16
SLC_BLOCK = 64
N_SELECT = 16
WINDOW = 512
ROPE_THETA = 10000.0
EPS = 1e-6
NEG_INF = -1e30
FORCE_SCORE = 1e9
LOG2E = 1.4426950408889634
MASK_SCORE = 1e30

LANES = 128
MXU_TILE = 256
VMEM_LIMIT = 56 * 1024 * 1024

BLK_Q = 0
BLK_KC, BLK_VC = 8, 10
BLK_KS, BLK_VS = 12, 14
BLK_KW, BLK_VW = 16, 18
BLK_GATE = 20
BLK_QS, BLK_KSB, BLK_VSB = 24, 32, 40
N_BLK = 48
PROJ_NB = 8
SLC_STEP = 8

BF16 = jnp.bfloat16
F32 = jnp.float32


def _nt(a, b):
    return lax.dot_general(a, b, (((1,), (1,)), ((), ())), preferred_element_type=F32)


def _mm(a, b):
    return jnp.dot(a, b, preferred_element_type=F32)


def _rms(v, g):
    return v * lax.rsqrt(jnp.mean(v * v, axis=-1, keepdims=True) + EPS) * g


def _ffn_kernel(x_ref, g_ref, wg_ref, wu_ref, wd_ref, o_ref, xn_ref):
    j = pl.program_id(1)

    @pl.when(j == 0)
    def _():
        xn_ref[...] = _rms(x_ref[...], g_ref[...]).astype(BF16)
        o_ref[...] = jnp.zeros_like(o_ref)

    xn = xn_ref[...]
    h = _mm(xn, wg_ref[...])
    u = _mm(xn, wu_ref[...])
    a = (h * (1.0 / (1.0 + jnp.exp(-h)))) * u
    o_ref[...] += _mm(a.astype(BF16), wd_ref[...])

    @pl.when(j == pl.num_programs(1) - 1)
    def _():
        o_ref[...] = x_ref[...] + 0.5 * o_ref[...]


def _ffn(x2d, gain, wg, wu, wd, tm, tf):
    t, d = x2d.shape
    ff = wg.shape[1]
    return pl.pallas_call(
        _ffn_kernel,
        grid=(t // tm, ff // tf),
        in_specs=[
            pl.BlockSpec((tm, d), lambda i, j: (i, 0)),
            pl.BlockSpec((1, d), lambda i, j: (0, 0)),
            pl.BlockSpec((d, tf), lambda i, j: (0, j)),
            pl.BlockSpec((d, tf), lambda i, j: (0, j)),
            pl.BlockSpec((tf, d), lambda i, j: (j, 0)),
        ],
        out_specs=pl.BlockSpec((tm, d), lambda i, j: (i, 0)),
        out_shape=jax.ShapeDtypeStruct((t, d), F32),
        scratch_shapes=[pltpu.VMEM((tm, d), BF16)],
        compiler_params=pltpu.CompilerParams(
            dimension_semantics=("parallel", "arbitrary"), vmem_limit_bytes=VMEM_LIMIT),
        name="ffn",
    )(x2d, gain.reshape(1, d), wg, wu, wd)


def _proj_kernel(x_ref, g_ref, w_ref, cos_ref, sin_ref, hg_ref, p_ref, c_ref, xn_ref):
    j = pl.program_id(2)
    scale = HEAD_DIM ** -0.5

    @pl.when(j == 0)
    def _():
        xn_ref[...] = _rms(x_ref[0], g_ref[...]).astype(BF16)

    acc = _mm(xn_ref[...], w_ref[...])

    def blk(c):
        return acc[:, c * LANES:(c + 1) * LANES]

    def rope(v):
        return v * cos_ref[0] + pltpu.roll(v, HEAD_DIM // 2, 1) * sin_ref[0]

    def put(c, v):
        p_ref[0, c] = v.astype(BF16)

    def finish(b, c):
        v = blk(c)
        if b < BLK_KC:
            put(c, rope(_rms(v, hg_ref[0:1, :])))
        elif b < BLK_KS:
            v = rope(v) if b < BLK_VC else v
            put(c, v)
            c_ref[0, b - BLK_KC] = v
        elif BLK_KS <= b < BLK_VS:
            put(c, rope(_rms(v, hg_ref[1:2, :])))
        elif BLK_KW <= b < BLK_VW:
            put(c, rope(_rms(v, hg_ref[2:3, :])))
        elif BLK_QS <= b < BLK_KSB:
            put(c, v * (-scale * LOG2E))
        else:
            put(c, v)

    for step in range(N_BLK // PROJ_NB):
        @pl.when(j == step)
        def _(step=step):
            for c in range(PROJ_NB):
                finish(step * PROJ_NB + c, c)


def _in_proj(x, gain, w, cosf, sinf, head_gains, tm):
    b, s, d = x.shape
    tn = PROJ_NB * LANES
    return pl.pallas_call(
        _proj_kernel,
        grid=(b, s // tm, N_BLK // PROJ_NB),
        in_specs=[
            pl.BlockSpec((1, tm, d), lambda bb, i, j: (bb, i, 0)),
            pl.BlockSpec((1, d), lambda bb, i, j: (0, 0)),
            pl.BlockSpec((d, tn), lambda bb, i, j: (0, j)),
            pl.BlockSpec((1, tm, LANES), lambda bb, i, j: (bb, i, 0)),
            pl.BlockSpec((1, tm, LANES), lambda bb, i, j: (bb, i, 0)),
            pl.BlockSpec((8, LANES), lambda bb, i, j: (0, 0)),
        ],
        out_specs=[
            pl.BlockSpec((1, PROJ_NB, tm, LANES), lambda bb, i, j: (bb, j, i, 0)),
            pl.BlockSpec((1, 4, tm, LANES), lambda bb, i, j: (bb, 0, i, 0)),
        ],
        out_shape=[
            jax.ShapeDtypeStruct((b, N_BLK, s, LANES), BF16),
            jax.ShapeDtypeStruct((b, 4, s, LANES), F32),
        ],
        scratch_shapes=[pltpu.VMEM((tm, d), BF16)],
        compiler_params=pltpu.CompilerParams(
            dimension_semantics=("parallel", "parallel", "arbitrary"), vmem_limit_bytes=VMEM_LIMIT),
        name="in_proj",
    )(x, gain.reshape(1, d), w, cosf, sinf, head_gains)


def _compress_kernel(x_ref, pos_ref, w1_ref, w2_ref, g_ref, o_ref, *, normalize):
    nb = o_ref.shape[2]
    hid = w2_ref.shape[0]
    acc = jnp.zeros((nb, 2 * hid), F32)
    bias = jnp.zeros((8, hid), F32)
    for l in range(CMP_STRIDE):
        rows = x_ref[0, 0, pl.ds(l, nb, stride=CMP_STRIDE), :]
        acc = acc + _mm(rows.astype(BF16), w1_ref[l])
        p_lo = jnp.broadcast_to(pos_ref[l:l + 1, :], (8, HEAD_DIM)).astype(BF16)
        p_hi = jnp.broadcast_to(pos_ref[l + CMP_STRIDE:l + CMP_STRIDE + 1, :], (8, HEAD_DIM)).astype(BF16)
        bias = bias + _mm(p_lo, w1_ref[l, :, :hid]) + _mm(p_hi, w1_ref[l, :, hid:])
    hidden = acc[:, :hid] + pltpu.roll(acc[:, hid:], nb - 1, 0) + bias[0:1, :]
    act = 0.5 * hidden * (1.0 + jnp.tanh(0.7978845608028654 * (hidden + 0.044715 * hidden * hidden * hidden)))
    out = _mm(act.astype(BF16), w2_ref[...])
    if normalize:
        out = _rms(out, g_ref[...])
    row = lax.broadcasted_iota(jnp.int32, out.shape, 0)
    o_ref[0, 0] = jnp.where(row < nb - 1, out, 0.0).astype(o_ref.dtype)


def _compress(src, first_blk, pos, w1cat, w2, gain, normalize, out_dtype):
    b, _, s, _ = src.shape
    nb = s // CMP_STRIDE
    hid = w2.shape[0]
    return pl.pallas_call(
        functools.partial(_compress_kernel, normalize=normalize),
        grid=(b, N_NSA_KV),
        in_specs=[
            pl.BlockSpec((1, 1, s, LANES), lambda bb, g: (bb, first_blk + g, 0, 0)),
            pl.BlockSpec((CMP_BLOCK, HEAD_DIM), lambda bb, g: (0, 0)),
            pl.BlockSpec((CMP_STRIDE, HEAD_DIM, 2 * hid), lambda bb, g: (0, 0, 0)),
            pl.BlockSpec((hid, HEAD_DIM), lambda bb, g: (0, 0)),
            pl.BlockSpec((1, HEAD_DIM), lambda bb, g: (0, 0)),
        ],
        out_specs=pl.BlockSpec((1, 1, nb, HEAD_DIM), lambda bb, g: (bb, g, 0, 0)),
        out_shape=jax.ShapeDtypeStruct((b, N_NSA_KV, nb, HEAD_DIM), out_dtype),
        compiler_params=pltpu.CompilerParams(
            dimension_semantics=("parallel", "parallel"), vmem_limit_bytes=VMEM_LIMIT),
        name="compress_k" if normalize else "compress_v",
    )(src, pos, w1cat, w2, gain.reshape(1, HEAD_DIM))


def _nsa_kernel(q_ref, kc_ref, vc_ref, ks_ref, vs_ref, kw_ref, vw_ref, gate_ref, gain_ref,
                o_ref, kaug_ref, vaug_ref, vwaug_ref, qaug_ref, m_ref, acc_ref, mix_ref, sc_ref, *, tq):
    i = pl.program_id(2)
    r = NSA_GROUP
    q0 = i * tq
    s_len = ks_ref.shape[2]
    ncb = kc_ref.shape[2]
    n_slc = s_len // SLC_BLOCK
    nsr = -(-n_slc // 8) * 8
    heads = range(r)

    @pl.when(i == 0)
    def _():
        lane = lax.broadcasted_iota(jnp.int32, (s_len, LANES), 1)
        kblk = lax.broadcasted_iota(jnp.int32, (s_len, LANES), 0) // SLC_BLOCK
        kaug_ref[:, :HEAD_DIM] = ks_ref[0, 0]
        kaug_ref[:, HEAD_DIM:] = jnp.where(lane == kblk, MASK_SCORE, 0.0).astype(BF16)
        ones = jnp.ones((s_len, LANES), BF16)
        vaug_ref[:, :HEAD_DIM] = vs_ref[0, 0]
        vaug_ref[:, HEAD_DIM:] = ones
        vwaug_ref[:, :HEAD_DIM] = vw_ref[0, 0]
        vwaug_ref[:, HEAD_DIM:] = ones

    gt = 1.0 / (1.0 + jnp.exp(-gate_ref[0, 0].astype(F32)))

    def gate(h, c):
        return gt[:, 3 * h + c:3 * h + c + 1]

    kc = kc_ref[0, 0].astype(BF16)
    vc = vc_ref[0, 0]
    cend = lax.broadcasted_iota(jnp.int32, (tq, ncb), 1) * CMP_STRIDE + (CMP_BLOCK - 1)
    cvalid = cend <= q0 + lax.broadcasted_iota(jnp.int32, (tq, ncb), 0)
    imp = jnp.zeros((tq, ncb), F32)
    for h in heads:
        s = jnp.where(cvalid, _nt(q_ref[0, h], kc), NEG_INF)
        e = jnp.where(cvalid, jnp.exp2(s - jnp.max(s, axis=-1, keepdims=True)), 0.0)
        den = jnp.sum(e, axis=-1, keepdims=True)
        pc = e * (1.0 / jnp.where(den > 0.0, den, 1.0))
        imp = imp + pc
        mix_ref[h * tq:(h + 1) * tq, :] = gate(h, 0) * _mm(pc.astype(BF16), vc)

    wwidth = WINDOW + tq
    wstart = jnp.maximum(q0 - WINDOW, 0)
    wrows = pl.ds(pl.multiple_of(wstart, tq), wwidth)
    kwin = kw_ref[0, 0, wrows, :]
    vwin = vwaug_ref[wrows, :]
    ahead = (lax.broadcasted_iota(jnp.int32, (tq, wwidth), 1)
             - lax.broadcasted_iota(jnp.int32, (tq, wwidth), 0)) + (wstart - q0)
    wmask = (ahead <= 0) & (ahead > -WINDOW)
    sw = [jnp.where(wmask, _nt(q_ref[0, h], kwin), NEG_INF) for h in heads]
    pw = [jnp.exp2(x - jnp.max(x, axis=-1, keepdims=True)).astype(BF16) for x in sw]
    ow = [_mm(x, vwin) for x in pw]
    for h in heads:
        mix_ref[h * tq:(h + 1) * tq, :] += gate(h, 2) * (ow[h][:, :HEAD_DIM] / ow[h][:, HEAD_DIM:])

    nsb = nsr
    imp_hi = imp.astype(BF16)
    imp_lo = (imp - imp_hi.astype(F32)).astype(BF16)
    brow = lax.broadcasted_iota(jnp.int32, (nsb, ncb), 0)
    ccol = lax.broadcasted_iota(jnp.int32, (nsb, ncb), 1)
    ratio = SLC_BLOCK // CMP_STRIDE
    span = CMP_BLOCK // CMP_STRIDE
    cover = ((ccol >= ratio * brow - (span - 1)) & (ccol <= ratio * brow + ratio - 1)
             & (ccol < ncb - 1)).astype(BF16)
    p_slc = _nt(cover, imp_hi) + _nt(cover, imp_lo)
    bidx = lax.broadcasted_iota(jnp.int32, (nsb, tq), 0)
    t_lane = q0 + lax.broadcasted_iota(jnp.int32, (nsb, tq), 1)
    cur = t_lane // SLC_BLOCK
    forced = (bidx == 0) | (bidx == cur) | (bidx == cur - 1)
    score = jnp.where(bidx <= cur, jnp.where(forced, FORCE_SCORE, p_slc), NEG_INF)
    score = jnp.where(bidx < n_slc, score, -3e38)
    sc_ref[...] = score
    ranks = []
    for c0 in range(0, nsb, 8):
        mine = score[c0:c0 + 8, :]
        brow8 = c0 + lax.broadcasted_iota(jnp.int32, (8, tq), 0)
        rank = jnp.zeros((8, tq), F32)
        for bp in range(n_slc):
            other = sc_ref[bp:bp + 1, :]
            ge = jnp.where(other >= mine, 1.0, 0.0)
            gt_ = jnp.where(other > mine, 1.0, 0.0)
            if bp < c0:
                rank = rank + ge
            elif bp >= c0 + 8:
                rank = rank + gt_
            else:
                rank = rank + jnp.where(brow8 > bp, ge, gt_)
        ranks.append(rank)
    rank = jnp.concatenate(ranks, axis=0)
    sel_t = jnp.where((rank < float(min(N_SELECT, n_slc))) & (bidx <= cur) & (bidx < n_slc), 0.0, -1.0)
    if nsb < LANES:
        sel_t = jnp.concatenate([sel_t, jnp.full((LANES - nsb, tq), -1.0, F32)], axis=0)
    selm1 = jnp.transpose(sel_t).astype(BF16)
    for h in heads:
        qaug_ref[h * tq:(h + 1) * tq, :HEAD_DIM] = q_ref[0, h]
        qaug_ref[h * tq:(h + 1) * tq, HEAD_DIM:] = selm1

    def init():
        m_ref[...] = jnp.full(m_ref.shape, NEG_INF, F32)
        acc_ref[...] = jnp.zeros(acc_ref.shape, F32)

    def flash(q_of, k, v, mask):
        width = k.shape[0]
        s = [_nt(q_of(h), k) for h in heads]
        p, alpha = [], []
        for h in heads:
            rows = pl.ds(h * tq, tq)
            sh = s[h] if mask is None else jnp.where(mask, s[h], NEG_INF)
            m_prev = m_ref[rows, :]
            m_next = jnp.maximum(m_prev, jnp.max(sh, axis=-1, keepdims=True))
            p.append(jnp.exp2(sh - jnp.tile(m_next, (1, width // LANES))).astype(BF16))
            alpha.append(jnp.exp2(m_prev - m_next))
            m_ref[rows, :] = m_next
        for h in heads:
            rows = pl.ds(h * tq, tq)
            acc_ref[rows, :] = jnp.tile(alpha[h], (1, 2)) * acc_ref[rows, :] + _mm(p[h], v)

    def result(h):
        rows = pl.ds(h * tq, tq)
        return acc_ref[rows, :HEAD_DIM] / acc_ref[rows, HEAD_DIM:]

    init()

    def q_slc(h):
        return qaug_ref[h * tq:(h + 1) * tq, :]

    def slc_body(j, carry):
        rows = pl.ds(pl.multiple_of(j * (SLC_STEP * tq), SLC_STEP * tq), SLC_STEP * tq)
        flash(q_slc, kaug_ref[rows, :], vaug_ref[rows, :], None)
        return carry

    lax.fori_loop(0, i // SLC_STEP, slc_body, 0)

    for left in range(SLC_STEP):
        @pl.when(i % SLC_STEP == left)
        def _(left=left):
            width = (left + 1) * tq
            rows = pl.ds(pl.multiple_of((i - left) * tq, tq), width)
            ahead = (lax.broadcasted_iota(jnp.int32, (tq, width), 1)
                     - lax.broadcasted_iota(jnp.int32, (tq, width), 0)) - left * tq
            flash(q_slc, kaug_ref[rows, :], vaug_ref[rows, :], ahead <= 0)

    for h in heads:
        merged = mix_ref[h * tq:(h + 1) * tq, :] + gate(h, 1) * result(h)
        o_ref[0, :, h * HEAD_DIM:(h + 1) * HEAD_DIM] = _rms(merged, gain_ref[0, h:h + 1, :]).astype(BF16)


def _nsa(p, kc, vc, gains, tq):
    b, _, s, _ = p.shape
    g, r = N_NSA_KV, NSA_GROUP
    ncb = kc.shape[2]
    nsr = -(-(s // SLC_BLOCK) // 8) * 8

    def kv_spec(first):
        return pl.BlockSpec((1, 1, s, LANES), lambda bb, gg, i: (bb, first + gg, 0, 0))

    return pl.pallas_call(
        functools.partial(_nsa_kernel, tq=tq),
        grid=(b, g, s // tq),
        in_specs=[
            pl.BlockSpec((1, r, tq, LANES), lambda bb, gg, i: (bb, gg, i, 0)),
            pl.BlockSpec((1, 1, ncb, LANES), lambda bb, gg, i: (bb, gg, 0, 0)),
            pl.BlockSpec((1, 1, ncb, LANES), lambda bb, gg, i: (bb, gg, 0, 0)),
            kv_spec(BLK_KS), kv_spec(BLK_VS), kv_spec(BLK_KW), kv_spec(BLK_VW),
            pl.BlockSpec((1, 1, tq, LANES), lambda bb, gg, i: (bb, BLK_GATE + gg, i, 0)),
            pl.BlockSpec((1, r, HEAD_DIM), lambda bb, gg, i: (gg, 0, 0)),
        ],
        out_specs=pl.BlockSpec((1, tq, r * HEAD_DIM), lambda bb, gg, i: (bb, i, gg)),
        out_shape=jax.ShapeDtypeStruct((b, s, N_NSA_HEADS * HEAD_DIM), BF16),
        scratch_shapes=[
            pltpu.VMEM((s, 2 * LANES), BF16),
            pltpu.VMEM((s, 2 * LANES), BF16),
            pltpu.VMEM((s, 2 * LANES), BF16),
            pltpu.VMEM((r * tq, 2 * LANES), BF16),
            pltpu.VMEM((r * tq, LANES), F32),
            pltpu.VMEM((r * tq, 2 * LANES), F32),
            pltpu.VMEM((r * tq, HEAD_DIM), F32),
            pltpu.VMEM((nsr, tq), F32),
        ],
        compiler_params=pltpu.CompilerParams(
            dimension_semantics=("parallel", "parallel", "arbitrary"), vmem_limit_bytes=VMEM_LIMIT),
        name="nsa",
    )(p, kc, vc, p, p, p, p, p, gains)


def _sb_kernel(q_ref, k_ref, v_ref, tri_ref, gain_ref, o_ref, acc_ref, c_ref, *, tk, parts):
    i = pl.program_id(2)
    acc_ref[...] = jnp.zeros(acc_ref.shape, F32)
    c_ref[...] = jnp.zeros(c_ref.shape, F32)
    tri = tri_ref[...]
    row = lax.broadcasted_iota(jnp.int32, (tk, tk), 0)
    col = lax.broadcasted_iota(jnp.int32, (tk, tk), 1)
    strict = col < row

    def tiles(k, v, work):
        rows = [pl.ds(part * tk, tk) for part, _ in work]
        zn = [_nt(q_ref[0, 0, r, :], k) for r in rows]
        lk = []
        for z, (_, masked) in zip(zn, work):
            x = jnp.minimum(z, 0.0) - jnp.log2(1.0 + jnp.exp2(-jnp.abs(z)))
            lk.append(jnp.where(strict, x, 0.0) if masked else x)
        incl = [_mm(x.astype(BF16), tri) for x in lk]
        a = []
        for n, z, r, (_, masked) in zip(incl, zn, rows, work):
            x = jnp.exp2(n + jnp.tile(c_ref[r, :], (1, tk // LANES)) - z)
            a.append((jnp.where(strict, x, 0.0) if masked else x).astype(BF16))
        for x, r in zip(a, rows):
            acc_ref[r, :] += _mm(x, v)
        for x, r in zip(lk, rows):
            c_ref[r, :] += jnp.sum(x, axis=-1, keepdims=True)

    def kv(j):
        k0 = pl.multiple_of(j * tk, tk)
        return k_ref[0, 0, pl.ds(k0, tk), :], v_ref[0, 0, pl.ds(k0, tk), :]

    for d in reversed(range(parts)):
        k, v = kv(parts * i + d)
        tiles(k, v, [(d, True)] + [(p, False) for p in range(d + 1, parts)])

    def body(jj, carry):
        k, v = kv(parts * i - 1 - jj)
        tiles(k, v, [(p, False) for p in range(parts)])
        return carry

    lax.fori_loop(0, parts * i, body, 0)
    o_ref[0] = _rms(acc_ref[...], gain_ref[0]).astype(BF16)


def _sb(p, tri, gains, tk, parts):
    b, _, s, _ = p.shape
    tq = parts * tk
    return pl.pallas_call(
        functools.partial(_sb_kernel, tk=tk, parts=parts),
        grid=(b, N_SB_HEADS, s // tq),
        in_specs=[
            pl.BlockSpec((1, 1, tq, LANES), lambda bb, h, i: (bb, BLK_QS + h, i, 0)),
            pl.BlockSpec((1, 1, s, LANES), lambda bb, h, i: (bb, BLK_KSB + h, 0, 0)),
            pl.BlockSpec((1, 1, s, LANES), lambda bb, h, i: (bb, BLK_VSB + h, 0, 0)),
            pl.BlockSpec((tk, tk), lambda bb, h, i: (0, 0)),
            pl.BlockSpec((1, 1, HEAD_DIM), lambda bb, h, i: (h, 0, 0)),
        ],
        out_specs=pl.BlockSpec((1, tq, HEAD_DIM), lambda bb, h, i: (bb, i, h)),
        out_shape=jax.ShapeDtypeStruct((b, s, N_SB_HEADS * HEAD_DIM), BF16),
        scratch_shapes=[
            pltpu.VMEM((tq, HEAD_DIM), F32),
            pltpu.VMEM((tq, LANES), F32),
        ],
        compiler_params=pltpu.CompilerParams(
            dimension_semantics=("parallel", "parallel", "arbitrary"), vmem_limit_bytes=VMEM_LIMIT),
        name="sb",
    )(p, p, p, tri, gains)


def _out_kernel(x_ref, ya_ref, yb_ref, wa_ref, wb_ref, o_ref):
    o_ref[...] = x_ref[...] + _mm(ya_ref[...], wa_ref[...]) + _mm(yb_ref[...], wb_ref[...])


def _out_proj(x2d, ya, yb, wa, wb, tm):
    t, d = x2d.shape
    ka, kb = ya.shape[1], yb.shape[1]
    return pl.pallas_call(
        _out_kernel,
        grid=(t // tm,),
        in_specs=[
            pl.BlockSpec((tm, d), lambda i: (i, 0)),
            pl.BlockSpec((tm, ka), lambda i: (i, 0)),
            pl.BlockSpec((tm, kb), lambda i: (i, 0)),
            pl.BlockSpec((ka, d), lambda i: (0, 0)),
            pl.BlockSpec((kb, d), lambda i: (0, 0)),
        ],
        out_specs=pl.BlockSpec((tm, d), lambda i: (i, 0)),
        out_shape=jax.ShapeDtypeStruct((t, d), F32),
        compiler_params=pltpu.CompilerParams(
            dimension_semantics=("parallel",), vmem_limit_bytes=VMEM_LIMIT),
        name="out_proj",
    )(x2d, ya, yb, wa, wb)


def _tile(n, pref):
    t = min(n, pref)
    while n % t:
        t //= 2
    return t


class _Plan(NamedTuple):
    tm_ffn: int
    tf: int
    tm_proj: int
    tm_out: int
    tq_nsa: int
    tk_sb: int
    parts_sb: int


def _plan(b, s, ff):
    plan = _Plan(tm_ffn=_tile(b * s, 1024), tf=_tile(ff, 512), tm_proj=_tile(s, 1024), tm_out=_tile(b * s, 512),
                 tq_nsa=MXU_TILE, tk_sb=MXU_TILE, parts_sb=max(1, min(16, s // MXU_TILE)))
    assert s % (plan.tk_sb * plan.parts_sb) == 0 and s % plan.tq_nsa == 0
    assert s // SLC_BLOCK <= LANES and WINDOW % plan.tq_nsa == 0 and s >= WINDOW + plan.tq_nsa
    return plan


def _pack_w_in(w_in):
    d = w_in.shape[0]
    hd = HEAD_DIM
    kv = N_NSA_KV * hd
    nq = N_NSA_HEADS * hd
    o_gate = nq + 6 * kv
    o_sb = o_gate + N_NSA_HEADS * 3
    gate = w_in[:, o_gate:o_sb].reshape(d, N_NSA_KV, NSA_GROUP * 3)
    gate = jnp.pad(gate, ((0, 0), (0, 0), (0, hd - NSA_GROUP * 3))).reshape(d, N_NSA_KV * hd)
    pad = jnp.zeros((d, (BLK_QS - BLK_GATE - N_NSA_KV) * hd), w_in.dtype)
    return jnp.concatenate([w_in[:, :o_gate], gate, pad, w_in[:, o_sb:]], axis=1).astype(BF16)


def kernel(x, positions, ffn1_norm, ffn1_w_gate, ffn1_w_up, ffn1_w_down, mix_norm, w_in,
           nsa_q_norm, nsa_k_cmp_norm, nsa_k_slc_norm, nsa_k_win_norm,
           cmp_k_pos, cmp_k_w1, cmp_k_w2, cmp_v_pos, cmp_v_w1, cmp_v_w2,
           nsa_out_norm, sb_out_norm, w_out, ffn2_norm, ffn2_w_gate, ffn2_w_up, ffn2_w_down):
    b, s, d = x.shape
    depth = w_in.shape[0]
    hd = HEAD_DIM
    hid = cmp_k_w2.shape[1]
    t = b * s
    plan = _plan(b, s, ffn1_w_gate.shape[2])

    inv_freq = ROPE_THETA ** (-jnp.arange(0, hd, 2, dtype=F32) / hd)
    ang = positions.astype(F32)[..., None] * inv_freq
    cosf = jnp.concatenate([jnp.cos(ang), jnp.cos(ang)], axis=-1)
    sinf = jnp.concatenate([-jnp.sin(ang), jnp.sin(ang)], axis=-1)

    idx = jnp.arange(plan.tk_sb, dtype=jnp.int32)
    tri = (idx[:, None] >= idx[None, :]).astype(BF16)

    def w1cat(w1):
        w = w1.reshape(CMP_BLOCK, hd, hid)
        return jnp.concatenate([w[:CMP_STRIDE], w[CMP_STRIDE:]], axis=-1).astype(BF16)

    for l in range(depth):
        x2 = _ffn(x.reshape(t, d), ffn1_norm[l], ffn1_w_gate[l].astype(BF16), ffn1_w_up[l].astype(BF16),
                  ffn1_w_down[l].astype(BF16), plan.tm_ffn, plan.tf)

        head_gains = jnp.zeros((8, hd), F32)
        head_gains = head_gains.at[0].set(nsa_q_norm[l] * (hd ** -0.5 * LOG2E)).at[1].set(nsa_k_slc_norm[l]).at[2].set(nsa_k_win_norm[l])
        p, csrc = _in_proj(x2.reshape(b, s, d), mix_norm[l], _pack_w_in(w_in[l]), cosf, sinf, head_gains,
                           plan.tm_proj)

        kc = _compress(csrc, 0, cmp_k_pos[l], w1cat(cmp_k_w1[l]), cmp_k_w2[l].astype(BF16),
                       nsa_k_cmp_norm[l], True, F32)
        vc = _compress(csrc, N_NSA_KV, cmp_v_pos[l], w1cat(cmp_v_w1[l]), cmp_v_w2[l].astype(BF16),
                       jnp.ones((hd,), F32), False, BF16)

        y_nsa = _nsa(p, kc, vc, nsa_out_norm[l].reshape(N_NSA_KV, NSA_GROUP, hd), plan.tq_nsa)
        y_sb = _sb(p, tri, sb_out_norm[l].reshape(N_SB_HEADS, 1, hd), plan.tk_sb, plan.parts_sb)

        nmix = N_NSA_HEADS * hd
        x3 = _out_proj(x2, y_nsa.reshape(t, nmix), y_sb.reshape(t, N_SB_HEADS * hd),
                       w_out[l, :nmix].astype(BF16), w_out[l, nmix:].astype(BF16), plan.tm_out)

        x = _ffn(x3, ffn2_norm[l], ffn2_w_gate[l].astype(BF16), ffn2_w_up[l].astype(BF16),
                 ffn2_w_down[l].astype(BF16), plan.tm_ffn, plan.tf).reshape(b, s, d)
    return x
```
